```python
import math
import jax, jax.numpy as jnp
from jax import lax
import numpy as np

D_MODEL = 4096
BATCH = 1
SEQ = 8192
DEPTH = 2

MIX_WIDTH = D_MODEL // 2
RET_HEAD_DIM = 256
RET_HEADS = MIX_WIDTH // RET_HEAD_DIM
RET_CHUNK = 128
ROPE_BASE = 10000.0
GDN_HEAD_DIM = 128
GDN_HEADS = MIX_WIDTH // GDN_HEAD_DIM
GDN_CHUNK = 64
CONV_WIDTH = 5
MLSTM_HEAD_DIM = 256
MLSTM_HEADS = MIX_WIDTH // MLSTM_HEAD_DIM
MLSTM_CHUNK = 128
N_BRANCHES = 3
D_FF = ((8 * D_MODEL // 3 + 255) // 256) * 256
EPS = 1e-6

IN_SIZES = (
    MIX_WIDTH, MIX_WIDTH, MIX_WIDTH, MIX_WIDTH,
    3 * MIX_WIDTH, MIX_WIDTH, 2 * GDN_HEADS, 2 * GDN_HEADS,
    MIX_WIDTH, MIX_WIDTH, MIX_WIDTH, MIX_WIDTH,
    2 * MLSTM_HEADS, 2 * MLSTM_HEADS,
    N_BRANCHES * D_MODEL,
)
IN_COLS = sum(IN_SIZES)

kernel_name = "bidir_hybrid_retention_gdn_mlstm_block"


def _rms_norm(x, w):
    xf = x.astype(jnp.float32)
    y = xf * lax.rsqrt(jnp.mean(xf * xf, axis=-1, keepdims=True) + EPS)
    return (y * w.astype(jnp.float32)).astype(x.dtype)


def _head_norm(y, w, n_heads, subtract_mean):
    if subtract_mean:
        y = y - jnp.mean(y, axis=-1, keepdims=True)
    y = y * lax.rsqrt(jnp.mean(y * y, axis=-1, keepdims=True) + EPS)
    return y * w.astype(jnp.float32).reshape(n_heads, 1, -1)


def _split_columns(p):
    offsets = np.cumsum(np.array(IN_SIZES))[:-1].tolist()
    return jnp.split(p, offsets, axis=-1)


def _heads(t, n_heads):
    b, s, _ = t.shape
    return t.astype(jnp.float32).reshape(b, s, n_heads, -1).transpose(0, 2, 1, 3)


def _merge_heads(t):
    b, h, s, d = t.shape
    return t.transpose(0, 2, 1, 3).reshape(b, s, h * d)


def _per_head_scalars(t, n_heads):
    b, s, _ = t.shape
    return t.astype(jnp.float32).reshape(b, s, 2, n_heads).transpose(2, 0, 3, 1)


def _to_chunks(t, chunk):
    b, h, s = t.shape[:3]
    t = t.reshape((b, h, s // chunk, chunk) + t.shape[3:])
    return jnp.moveaxis(t, 2, 0)


def _from_chunks(t):
    n, b, h, c = t.shape[:4]
    return jnp.moveaxis(t, 0, 2).reshape((b, h, n * c) + t.shape[4:])


def _flip(t):
    return jnp.flip(t, axis=2)


def _rotary(t, positions):
    half = t.shape[-1] // 2
    inv_freq = ROPE_BASE ** (-jnp.arange(half, dtype=jnp.float32) / half)
    ang = positions.astype(jnp.float32)[:, None, :, None] * inv_freq
    cos, sin = jnp.cos(ang), jnp.sin(ang)
    t1, t2 = t[..., :half], t[..., half:]
    return jnp.concatenate([t1 * cos - t2 * sin, t1 * sin + t2 * cos], axis=-1)


def _l2_normalize(t):
    return t * lax.rsqrt(jnp.sum(t * t, axis=-1, keepdims=True) + EPS)


def _centred_depthwise_conv(x, w):
    pad = (CONV_WIDTH - 1) // 2
    return lax.conv_general_dilated(
        x, w[:, None, :].astype(x.dtype), window_strides=(1,), padding=[(pad, pad)],
        dimension_numbers=("NWC", "WIO", "NWC"), feature_group_count=x.shape[-1])


def _retention_scan(q, k, v, log_gamma, inclusive):
    b, h, _, dk = q.shape
    dv = v.shape[-1]
    c = RET_CHUNK
    idx = jnp.arange(c, dtype=jnp.float32)
    diff = idx[:, None] - idx[None, :]
    mask = (diff >= 0) if inclusive else (diff > 0)
    lg = log_gamma[:, None, None]
    decay_intra = jnp.where(mask, jnp.exp(lg * jnp.where(mask, diff, 0.0)), 0.0)
    q_decay = jnp.exp(log_gamma[:, None] * (idx + 1.0))[..., None]
    k_decay = jnp.exp(log_gamma[:, None] * (c - 1.0 - idx))[..., None]
    chunk_decay = jnp.exp(log_gamma * c)[:, None, None]

    def step(state, inp):
        qc, kc, vc = inp
        scores = jnp.einsum("bhid,bhjd->bhij", qc, kc) * decay_intra
        out = (jnp.einsum("bhij,bhjv->bhiv", scores, vc)
               + jnp.einsum("bhid,bhdv->bhiv", qc * q_decay, state))
        state = state * chunk_decay + jnp.einsum("bhjd,bhjv->bhdv", kc * k_decay, vc)
        return state, out

    state0 = jnp.zeros((b, h, dk, dv), jnp.float32)
    _, out = lax.scan(step, state0, tuple(_to_chunks(t, c) for t in (q, k, v)))
    return _from_chunks(out)


def _retention_branch(q, k, v, g, positions, norm_w):
    q = _rotary(_heads(q, RET_HEADS), positions)
    k = _rotary(_heads(k, RET_HEADS), positions) * (RET_HEAD_DIM ** -0.5)
    v = _heads(v, RET_HEADS)
    log_gamma = jnp.log1p(-jnp.exp2(-5.0 - jnp.arange(RET_HEADS, dtype=jnp.float32)))
    y = (_retention_scan(q, k, v, log_gamma, True)
         + _flip(_retention_scan(_flip(q), _flip(k), _flip(v), log_gamma, False)))
    y = _head_norm(y, norm_w, RET_HEADS, subtract_mean=True)
    y = _merge_heads(y) * jax.nn.silu(g.astype(jnp.float32))
    return y.astype(g.dtype)


def _gated_delta_scan(q, k, v, log_alpha, beta):
    b, h, _, dk = q.shape
    dv = v.shape[-1]
    c = GDN_CHUNK
    incl = jnp.tril(jnp.ones((c, c), dtype=bool))
    strict = jnp.tril(jnp.ones((c, c), dtype=bool), -1)
    eye = jnp.eye(c, dtype=jnp.float32)

    def step(state, inp):
        qc, kc, vc, gc, bc = inp
        gcum = jnp.cumsum(gc, axis=-1)
        diff = gcum[..., :, None] - gcum[..., None, :]
        decay = jnp.where(incl, jnp.exp(jnp.where(incl, diff, 0.0)), 0.0)
        k_beta = kc * bc[..., None]
        lower = jnp.einsum("bhid,bhjd->bhij", k_beta, kc) * jnp.where(strict, decay, 0.0)
        rhs = jnp.concatenate([vc * bc[..., None], k_beta * jnp.exp(gcum)[..., None]], axis=-1)
        sol = lax.linalg.triangular_solve(eye + lower, rhs, left_side=True, lower=True)
        u, w = sol[..., :dv], sol[..., dv:]
        v_new = u - jnp.einsum("bhik,bhkv->bhiv", w, state)
        attn = jnp.einsum("bhid,bhjd->bhij", qc, kc) * decay
        out = (jnp.einsum("bhij,bhjv->bhiv", attn, v_new)
               + jnp.einsum("bhik,bhkv->bhiv", qc * jnp.exp(gcum)[..., None], state))
        g_last = gcum[..., -1]
        state = (state * jnp.exp(g_last)[..., None, None]
                 + jnp.einsum("bhjk,bhjv->bhkv", kc * jnp.exp(g_last[..., None] - gcum)[..., None], v_new))
        return state, out

    state0 = jnp.zeros((b, h, dk, dv), jnp.float32)
    _, out = lax.scan(step, state0, tuple(_to_chunks(t, c) for t in (q, k, v, log_alpha, beta)))
    return _from_chunks(out)


def _gdn_branch(qkv, g_out, beta_pre, alpha_pre, conv_w, a_log, dt_bias, norm_w):
    qkv = jax.nn.silu(_centred_depthwise_conv(qkv, conv_w))
    q, k, v = jnp.split(qkv, 3, axis=-1)
    q = _l2_normalize(_heads(q, GDN_HEADS)) * (GDN_HEAD_DIM ** -0.5)
    k = _l2_normalize(_heads(k, GDN_HEADS))
    v = _heads(v, GDN_HEADS)
    beta = jax.nn.sigmoid(_per_head_scalars(beta_pre, GDN_HEADS))
    log_alpha = (-jnp.exp(a_log.astype(jnp.float32))[:, None, :, None]
                 * jax.nn.softplus(_per_head_scalars(alpha_pre, GDN_HEADS)
                                   + dt_bias.astype(jnp.float32)[:, None, :, None]))
    y = (_gated_delta_scan(q, k, v, log_alpha[0], beta[0])
         + _flip(_gated_delta_scan(_flip(q), _flip(k), _flip(v), _flip(log_alpha[1]), _flip(beta[1]))))
    y = _head_norm(y, norm_w, GDN_HEADS, subtract_mean=False)
    y = _merge_heads(y) * jax.nn.silu(g_out.astype(jnp.float32))
    return y.astype(g_out.dtype)


def _mlstm_scan(q, k, v, log_i, log_f):
    b, h, _, dk = q.shape
    dv = v.shape[-1]
    c = MLSTM_CHUNK
    incl = jnp.tril(jnp.ones((c, c), dtype=bool))

    def step(carry, inp):
        mem, nrm, m_prev = carry
        qc, kc, vc, ic, fc = inp
        bcum = jnp.cumsum(fc, axis=-1)
        log_d = jnp.where(incl, bcum[..., :, None] - bcum[..., None, :] + ic[..., None, :], -jnp.inf)
        m_inter = bcum + m_prev[..., None]
        m_row = jnp.maximum(m_inter, jnp.max(log_d, axis=-1))
        s = jnp.einsum("bhid,bhjd->bhij", qc, kc) * jnp.exp(log_d - m_row[..., None])
        inter_scale = jnp.exp(m_inter - m_row)
        num = (jnp.einsum("bhij,bhjv->bhiv", s, vc)
               + inter_scale[..., None] * jnp.einsum("bhid,bhdv->bhiv", qc, mem))
        den = jnp.sum(s, axis=-1) + inter_scale * jnp.einsum("bhid,bhd->bhi", qc, nrm)
        hc = num / jnp.maximum(jnp.abs(den), jnp.exp(-m_row))[..., None]
        b_last = bcum[..., -1]
        log_w = b_last[..., None] - bcum + ic
        m_new = jnp.maximum(b_last + m_prev, jnp.max(log_w, axis=-1))
        w = jnp.exp(log_w - m_new[..., None])[..., None]
        carry_scale = jnp.exp(b_last + m_prev - m_new)
        mem = carry_scale[..., None, None] * mem + jnp.einsum("bhjd,bhjv->bhdv", kc * w, vc)
        nrm = carry_scale[..., None] * nrm + jnp.sum(kc * w, axis=2)
        return (mem, nrm, m_new), hc

    carry0 = (jnp.zeros((b, h, dk, dv), jnp.float32), jnp.zeros((b, h, dk), jnp.float32),
              jnp.zeros((b, h), jnp.float32))
    _, out = lax.scan(step, carry0, tuple(_to_chunks(t, c) for t in (q, k, v, log_i, log_f)))
    return _from_chunks(out)


def _mlstm_branch(q, k, v, o, i_pre, f_pre, i_bias, f_bias, norm_w):
    q = _heads(q, MLSTM_HEADS)
    k = _heads(k, MLSTM_HEADS) * (MLSTM_HEAD_DIM ** -0.5)
    v = _heads(v, MLSTM_HEADS)
    log_i = _per_head_scalars(i_pre, MLSTM_HEADS) + i_bias.astype(jnp.float32)[:, None, :, None]
    log_f = jax.nn.log_sigmoid(_per_head_scalars(f_pre, MLSTM_HEADS)
                               + f_bias.astype(jnp.float32)[:, None, :, None])
    y = (_mlstm_scan(q, k, v, log_i[0], log_f[0])
         + _flip(_mlstm_scan(_flip(q), _flip(k), _flip(v), _flip(log_i[1]), _flip(log_f[1]))))
    y = _head_norm(y, norm_w, MLSTM_HEADS, subtract_mean=False)
    y = _merge_heads(y) * jax.nn.sigmoid(o.astype(jnp.float32))
    return y.astype(o.dtype)


def setup_inputs(seed: int = 0) -> dict:
    key = jax.random.key(seed)
    ks = jax.random.split(key, 20)
    f32 = jnp.float32
    x = jax.random.normal(ks[0], (BATCH, SEQ, D_MODEL), f32)
    positions = jnp.broadcast_to(jnp.arange(SEQ, dtype=jnp.int32), (BATCH, SEQ))
    norm_mix = 1.0 + 0.02 * jax.random.normal(ks[1], (DEPTH, D_MODEL), f32)
    w_in = jax.random.normal(ks[2], (DEPTH, D_MODEL, IN_COLS), f32) * D_MODEL ** -0.5
    conv_w = jax.random.normal(ks[3], (DEPTH, CONV_WIDTH, 3 * MIX_WIDTH), f32) * CONV_WIDTH ** -0.5
    gdn_a_log = jnp.log(jax.random.uniform(ks[4], (DEPTH, 2, GDN_HEADS), f32, 1.0, 16.0))
    dt = jnp.exp(jax.random.uniform(ks[5], (DEPTH, 2, GDN_HEADS), f32, math.log(1e-3), math.log(1e-1)))
    gdn_dt_bias = dt + jnp.log(-jnp.expm1(-dt))
    mlstm_i_bias = 0.1 * jax.random.normal(ks[6], (DEPTH, 2, MLSTM_HEADS), f32)
    mlstm_f_bias = (jnp.linspace(3.0, 6.0, MLSTM_HEADS, dtype=f32)
                    + 0.1 * jax.random.normal(ks[7], (DEPTH, 2, MLSTM_HEADS), f32))
    ret_norm = 1.0 + 0.02 * jax.random.normal(ks[8], (DEPTH, MIX_WIDTH), f32)
    gdn_norm = 1.0 + 0.02 * jax.random.normal(ks[9], (DEPTH, MIX_WIDTH), f32)
    mlstm_norm = 1.0 + 0.02 * jax.random.normal(ks[10], (DEPTH, MIX_WIDTH), f32)
    w_branch = jax.random.normal(ks[11], (DEPTH, N_BRANCHES, MIX_WIDTH, D_MODEL), f32) * MIX_WIDTH ** -0.5
    w_out = jax.random.normal(ks[12], (DEPTH, D_MODEL, D_MODEL), f32) * D_MODEL ** -0.5
    norm_ffn = 1.0 + 0.02 * jax.random.normal(ks[13], (DEPTH, D_MODEL), f32)
    w_ffn_in = jax.random.normal(ks[14], (DEPTH, D_MODEL, 2 * D_FF), f32) * D_MODEL ** -0.5
    w_ffn_out = jax.random.normal(ks[15], (DEPTH, D_FF, D_MODEL), f32) * D_FF ** -0.5
    norm_final = 1.0 + 0.02 * jax.random.normal(ks[16], (D_MODEL,), f32)
    return {"x": x, "positions": positions, "norm_mix": norm_mix, "w_in": w_in, "conv_w": conv_w,
            "gdn_a_log": gdn_a_log, "gdn_dt_bias": gdn_dt_bias, "mlstm_i_bias": mlstm_i_bias,
            "mlstm_f_bias": mlstm_f_bias, "ret_norm": ret_norm, "gdn_norm": gdn_norm,
            "mlstm_norm": mlstm_norm, "w_branch": w_branch, "w_out": w_out, "norm_ffn": norm_ffn,
            "w_ffn_in": w_ffn_in, "w_ffn_out": w_ffn_out, "norm_final": norm_final}


def reference(x, positions, norm_mix, w_in, conv_w, gdn_a_log, gdn_dt_bias, mlstm_i_bias,
              mlstm_f_bias, ret_norm, gdn_norm, mlstm_norm, w_branch, w_out, norm_ffn,
              w_ffn_in, w_ffn_out, norm_final):
    b, s, _ = x.shape
    for l in range(DEPTH):
        h = _rms_norm(x, norm_mix[l])
        p = jnp.einsum("bsd,dc->bsc", h, w_in[l])
        (r_q, r_k, r_v, r_g, g_qkv, g_out, g_beta, g_alpha,
         m_q, m_k, m_v, m_o, m_i, m_f, merge_logits) = _split_columns(p)
        y_ret = _retention_branch(r_q, r_k, r_v, r_g, positions, ret_norm[l])
        y_gdn = _gdn_branch(g_qkv, g_out, g_beta, g_alpha, conv_w[l], gdn_a_log[l],
                            gdn_dt_bias[l], gdn_norm[l])
        y_mlstm = _mlstm_branch(m_q, m_k, m_v, m_o, m_i, m_f, mlstm_i_bias[l],
                                mlstm_f_bias[l], mlstm_norm[l])
        branches = jnp.stack([y_ret, y_gdn, y_mlstm], axis=2)
        lifted = jnp.einsum("bsrc,rcd->bsrd", branches, w_branch[l])
        gates = jax.nn.sigmoid(merge_logits.reshape(b, s, N_BRANCHES, D_MODEL))
        merged = jnp.sum(gates * lifted, axis=2)
        x = x + jnp.einsum("bsd,de->bse", merged, w_out[l])
        h = _rms_norm(x, norm_ffn[l])
        gate, up = jnp.split(jnp.einsum("bsd,df->bsf", h, w_ffn_in[l]), 2, axis=-1)
        x = x + jnp.einsum("bsf,fd->bsd", jax.nn.silu(gate) * up, w_ffn_out[l])
    return _rms_norm(x, norm_final)
```

```python
import functools
import math

import numpy as np
import jax
import jax.numpy as jnp
from jax import lax
from jax.experimental import pallas as pl
from jax.experimental.pallas import tpu as pltpu

RET_HEAD_DIM = 256
GDN_HEAD_DIM = 128
MLSTM_HEAD_DIM = 256
CONV_WIDTH = 5
ROPE_BASE = 10000.0
N_BRANCHES = 3
EPS = 1e-6

CHUNK = 128
GATE_ROWS = 128
V7X_VMEM_BYTES = 64 * 1024 * 1024
VMEM_LIMIT = V7X_VMEM_BYTES - 8 * 1024 * 1024

F32 = jnp.float32
BF16 = jnp.bfloat16


def _params(*sem):
    return pltpu.CompilerParams(dimension_semantics=sem, vmem_limit_bytes=VMEM_LIMIT)


def _pick(n, candidates):
    for c in candidates:
        if n % c == 0:
            return c
    raise ValueError(f"no tile in {candidates} divides {n}")


def _dot(a, b):
    return jnp.dot(a, b, preferred_element_type=F32)


def _dot_nt(a, b):
    return lax.dot_general(a, b, (((1,), (1,)), ((), ())), preferred_element_type=F32)


def _dot_tn(a, b):
    return lax.dot_general(a, b, (((0,), (0,)), ((), ())), preferred_element_type=F32)


def _dot_f32(a, b):
    return jnp.dot(a, b, preferred_element_type=F32, precision=lax.Precision.HIGHEST)


def _row_to_col(row):
    c = row.shape[-1]
    i = lax.broadcasted_iota(jnp.int32, (c, c), 0)
    j = lax.broadcasted_iota(jnp.int32, (c, c), 1)
    return jnp.sum(jnp.where(i == j, jnp.broadcast_to(row, (c, c)), 0.0), axis=1, keepdims=True)


def _lane_pick(row, idx):
    lane = lax.broadcasted_iota(jnp.int32, row.shape, 1)
    return jnp.sum(jnp.where(lane == idx, row, 0.0), axis=1, keepdims=True)


def _dir_masks(c, d):
    i = lax.broadcasted_iota(jnp.int32, (c, c), 0)
    j = lax.broadcasted_iota(jnp.int32, (c, c), 1)
    ahead = (i - j) * jnp.where(d == 0, 1, -1)
    return ahead >= 0, ahead > 0


def _rmsnorm_kernel(x_ref, w_ref, o_ref):
    x = x_ref[...]
    y = x * lax.rsqrt(jnp.mean(x * x, axis=-1, keepdims=True) + EPS)
    o_ref[...] = (y * w_ref[...]).astype(o_ref.dtype)


def _rmsnorm(x, w, out_dtype):
    s, d = x.shape
    tm = _pick(s, (256, 128))
    return pl.pallas_call(
        _rmsnorm_kernel,
        grid=(s // tm,),
        in_specs=[pl.BlockSpec((tm, d), lambda i: (i, 0)), pl.BlockSpec((1, d), lambda i: (0, 0))],
        out_specs=pl.BlockSpec((tm, d), lambda i: (i, 0)),
        out_shape=jax.ShapeDtypeStruct((s, d), out_dtype),
        compiler_params=_params("parallel"),
        name="rmsnorm",
    )(x, w.reshape(1, d))


def _mm_kernel(a_ref, b_ref, o_ref):
    o_ref[...] = _dot(a_ref[...], b_ref[...]).astype(o_ref.dtype)


def _matmul_w_resident(a, b, out_dtype):
    m, k = a.shape
    n = b.shape[1]
    tm = _pick(m, (512, 256, 128))
    tn = _pick(n, (1024, 512, 256, 128))
    return pl.pallas_call(
        _mm_kernel,
        grid=(n // tn, m // tm),
        in_specs=[pl.BlockSpec((tm, k), lambda j, i: (i, 0)), pl.BlockSpec((k, tn), lambda j, i: (0, j))],
        out_specs=pl.BlockSpec((tm, tn), lambda j, i: (i, j)),
        out_shape=jax.ShapeDtypeStruct((m, n), out_dtype),
        compiler_params=_params("parallel", "parallel"),
        name="in_proj",
    )(a, b)


def _mm_res_kernel(a_ref, b_ref, r_ref, o_ref, acc_ref):
    kk = pl.program_id(2)

    @pl.when(kk == 0)
    def _():
        acc_ref[...] = r_ref[...]

    acc_ref[...] += _dot(a_ref[...], b_ref[...])

    @pl.when(kk == pl.num_programs(2) - 1)
    def _():
        o_ref[...] = acc_ref[...]


def _matmul_residual(a, b, res, name):
    m, k = a.shape
    n = b.shape[1]
    tm = _pick(m, (512, 256, 128))
    tn = _pick(n, (1024, 512, 256, 128))
    gk = 1
    while k // gk > 6144 and k % (2 * gk) == 0 and (k // (2 * gk)) % 128 == 0:
        gk *= 2
    tk = k // gk
    return pl.pallas_call(
        _mm_res_kernel,
        grid=(m // tm, n // tn, gk),
        in_specs=[pl.BlockSpec((tm, tk), lambda i, j, kk: (i, kk)),
                  pl.BlockSpec((tk, tn), lambda i, j, kk: (kk, j)),
                  pl.BlockSpec((tm, tn), lambda i, j, kk: (i, j))],
        out_specs=pl.BlockSpec((tm, tn), lambda i, j, kk: (i, j)),
        out_shape=jax.ShapeDtypeStruct((m, n), F32),
        scratch_shapes=[pltpu.VMEM((tm, tn), F32)],
        compiler_params=_params("parallel", "parallel", "arbitrary"),
        name=name,
    )(a, b, res)


def _swiglu_kernel(a_ref, wg_ref, wu_ref, o_ref):
    a = a_ref[...]
    gate = _dot(a, wg_ref[...])
    up = _dot(a, wu_ref[...])
    o_ref[...] = (gate * jax.nn.sigmoid(gate) * up).astype(o_ref.dtype)


def _ffn_in(h, w, d_ff):
    m, k = h.shape
    tm = _pick(m, (1024, 512, 256, 128))
    tn = _pick(d_ff, (512, 256, 128))
    nb = d_ff // tn
    return pl.pallas_call(
        _swiglu_kernel,
        grid=(m // tm, nb),
        in_specs=[pl.BlockSpec((tm, k), lambda i, j: (i, 0)),
                  pl.BlockSpec((k, tn), lambda i, j: (0, j)),
                  pl.BlockSpec((k, tn), lambda i, j: (0, j + nb))],
        out_specs=pl.BlockSpec((tm, tn), lambda i, j: (i, j)),
        out_shape=jax.ShapeDtypeStruct((m, d_ff), BF16),
        compiler_params=_params("parallel", "parallel"),
        name="ffn_in",
    )(h, w, w)


def _gates_kernel(wt_ref, h_ref, par_ref, o_ref, *, n_gdn, n_mlstm):
    z = _dot_nt(wt_ref[...], h_ref[...])
    r, tm = z.shape
    bias = par_ref[:, 0:1]
    scale = par_ref[:, 1:2]
    row = lax.broadcasted_iota(jnp.int32, (r, tm), 0)
    zb = z + bias
    soft = jnp.log1p(jnp.exp(-jnp.abs(zb)))
    softplus = jnp.maximum(zb, 0.0) + soft
    log_sig = jnp.minimum(zb, 0.0) - soft
    b0, b1, b2, b3 = 2 * n_gdn, 4 * n_gdn, 4 * n_gdn + 2 * n_mlstm, 4 * n_gdn + 4 * n_mlstm
    is_alpha = (row >= b0) & (row < b1)
    is_logf = (row >= b2) & (row < b3)
    raw = jnp.where(row < b0, jax.nn.sigmoid(zb),
                    jnp.where(is_alpha, scale * softplus,
                              jnp.where(row < b2, zb, jnp.where(is_logf, log_sig, 0.0))))
    pj = lax.broadcasted_iota(jnp.int32, (tm, tm), 0)
    pi = lax.broadcasted_iota(jnp.int32, (tm, tm), 1)
    same = (pj // CHUNK) == (pi // CHUNK)
    prefix = jnp.where(same & (pj <= pi), 1.0, 0.0)
    suffix = jnp.where(same & (pj >= pi), 1.0, 0.0)
    cum_p = _dot_f32(raw, prefix)
    cum_s = _dot_f32(raw, suffix)
    dir1 = (is_alpha & (row >= b0 + n_gdn)) | (is_logf & (row >= b2 + n_mlstm))
    cum = jnp.where(dir1, cum_s, cum_p)
    o_ref[...] = jnp.where(is_alpha | is_logf, cum, raw)


def _gates(h, wt, par, n_gdn, n_mlstm):
    s, d = h.shape
    tm = _pick(s, (256, 128))
    return pl.pallas_call(
        functools.partial(_gates_kernel, n_gdn=n_gdn, n_mlstm=n_mlstm),
        grid=(s // tm,),
        in_specs=[pl.BlockSpec((GATE_ROWS, d), lambda i: (0, 0)),
                  pl.BlockSpec((tm, d), lambda i: (i, 0)),
                  pl.BlockSpec((GATE_ROWS, 2), lambda i: (0, 0))],
        out_specs=pl.BlockSpec((GATE_ROWS, tm), lambda i: (0, i)),
        out_shape=jax.ShapeDtypeStruct((GATE_ROWS, s), F32),
        compiler_params=_params("parallel"),
        name="gates",
    )(wt, h, par)


def _rope_kernel(pos_ref, freq_ref, cos_ref, sin_ref):
    ang = pos_ref[...].astype(F32) * freq_ref[...]
    cos_ref[...] = jnp.cos(ang)
    sin_ref[...] = jnp.sin(ang)


def _rope_table(positions, half):
    s = positions.shape[0]
    tm = _pick(s, (512, 256, 128))
    inv_freq = (ROPE_BASE ** (-jnp.arange(half, dtype=F32) / half)).reshape(1, half)
    return pl.pallas_call(
        _rope_kernel,
        grid=(s // tm,),
        in_specs=[pl.BlockSpec((tm, 1), lambda i: (i, 0)), pl.BlockSpec((1, half), lambda i: (0, 0))],
        out_specs=[pl.BlockSpec((tm, half), lambda i: (i, 0))] * 2,
        out_shape=[jax.ShapeDtypeStruct((s, half), F32)] * 2,
        compiler_params=_params("parallel"),
        name="rope_table",
    )(positions.reshape(s, 1), inv_freq)


def _chunk_index(d, c, nc):
    return jnp.where(d == 0, c, nc - 1 - c)


def _retention_kernel(lg_ref, q_ref, k_ref, v_ref, cos_ref, sin_ref, o_ref, state_ref):
    d = pl.program_id(0)
    h = pl.program_id(1)

    @pl.when(pl.program_id(2) == 0)
    def _():
        state_ref[...] = jnp.zeros_like(state_ref)

    c, dk = q_ref.shape
    half = dk // 2
    cos = cos_ref[...]
    sin = sin_ref[...]

    def rot(t):
        t1, t2 = t[:, :half], t[:, half:]
        return jnp.concatenate([t1 * cos - t2 * sin, t1 * sin + t2 * cos], axis=1)

    q = rot(q_ref[...])
    k = rot(k_ref[...]) * (dk ** -0.5)
    v = v_ref[...].astype(BF16)
    lg = jnp.full((1, 1), lg_ref[h], F32)
    fwd = d == 0
    i = lax.broadcasted_iota(jnp.int32, (c, c), 0)
    j = lax.broadcasted_iota(jnp.int32, (c, c), 1)
    ahead = (i - j) * jnp.where(fwd, 1, -1)
    mask = ahead >= jnp.where(fwd, 0, 1)
    dist = jnp.abs(i - j).astype(F32)
    decay_intra = jnp.where(mask, jnp.exp(lg * jnp.where(mask, dist, 0.0)), 0.0)
    idx = lax.broadcasted_iota(jnp.int32, (c, 1), 0).astype(F32)
    q_decay = jnp.exp(lg * jnp.where(fwd, idx + 1.0, c - idx))
    k_decay = jnp.exp(lg * jnp.where(fwd, c - 1.0 - idx, idx))
    chunk_decay = jnp.exp(lg * c)

    state = state_ref[...]
    scores = _dot_nt(q.astype(BF16), k.astype(BF16)) * decay_intra
    out = _dot(scores.astype(BF16), v) + _dot((q * q_decay).astype(BF16), state.astype(BF16))
    state_ref[...] = state * chunk_decay + _dot_tn((k * k_decay).astype(BF16), v)
    o_ref[...] = out


def _retention(p, offs, positions_tables, n_heads):
    s = p.shape[0]
    dk = RET_HEAD_DIM
    nc = s // CHUNK
    cos, sin = positions_tables
    qo, ko, vo = (o // dk for o in offs)
    log_gamma = jnp.log1p(-jnp.exp2(-5.0 - jnp.arange(n_heads, dtype=F32)))

    def col(o):
        return pl.BlockSpec((CHUNK, dk), lambda d, h, c: (_chunk_index(d, c, nc), o + h))

    tab = pl.BlockSpec((CHUNK, dk // 2), lambda d, h, c: (_chunk_index(d, c, nc), 0))
    return pl.pallas_call(
        _retention_kernel,
        grid=(2, n_heads, nc),
        in_specs=[pl.BlockSpec(memory_space=pltpu.SMEM), col(qo), col(ko), col(vo), tab, tab],
        out_specs=pl.BlockSpec((None, CHUNK, dk), lambda d, h, c: (d, _chunk_index(d, c, nc), h)),
        out_shape=jax.ShapeDtypeStruct((2, s, n_heads * dk), F32),
        scratch_shapes=[pltpu.VMEM((dk, dk), F32)],
        compiler_params=_params("parallel", "parallel", "arbitrary"),
        name="retention_scan",
    )(log_gamma, p, p, p, cos, sin)


def _conv_kernel(prev_ref, cur_ref, next_ref, w_ref, o_ref, pad_ref, *, n_qk_blocks, n_q_blocks, halo):
    i = pl.program_id(0)
    jb = pl.program_id(1)
    tr, tc = cur_ref.shape
    keep_prev = jnp.where(i == 0, 0.0, 1.0)
    keep_next = jnp.where(i == pl.num_programs(0) - 1, 0.0, 1.0)
    pad_ref[0:halo, :] = prev_ref[...] * keep_prev
    pad_ref[halo:halo + tr, :] = cur_ref[...]
    pad_ref[halo + tr:, :] = next_ref[...] * keep_next
    side = (CONV_WIDTH - 1) // 2
    acc = jnp.zeros((tr, tc), F32)
    for tap in range(CONV_WIDTH):
        acc = acc + pad_ref[pl.ds(halo - side + tap, tr), :] * w_ref[tap:tap + 1, :]
    y = acc * jax.nn.sigmoid(acc)
    is_qk = jb < n_qk_blocks
    q_scale = jnp.where(jb < n_q_blocks, GDN_HEAD_DIM ** -0.5, 1.0)
    for g in range(tc // GDN_HEAD_DIM):
        sl = slice(g * GDN_HEAD_DIM, (g + 1) * GDN_HEAD_DIM)
        t = y[:, sl]
        normed = t * lax.rsqrt(jnp.sum(t * t, axis=-1, keepdims=True) + EPS) * q_scale
        o_ref[:, sl] = jnp.where(is_qk, normed, t)


def _gdn_conv(p, off, width, conv_w):
    s = p.shape[0]
    halo = 8
    tr = _pick(s, (512, 256, 128))
    tc = _pick(width // 3, (512, 256, 128))
    ob = off // tc
    n_row8 = s // halo
    return pl.pallas_call(
        functools.partial(_conv_kernel, n_qk_blocks=2 * (width // 3) // tc, n_q_blocks=(width // 3) // tc,
                          halo=halo),
        grid=(s // tr, width // tc),
        in_specs=[pl.BlockSpec((halo, tc), lambda i, j: (jnp.maximum(i * (tr // halo) - 1, 0), ob + j)),
                  pl.BlockSpec((tr, tc), lambda i, j: (i, ob + j)),
                  pl.BlockSpec((halo, tc), lambda i, j: (jnp.minimum((i + 1) * (tr // halo), n_row8 - 1), ob + j)),
                  pl.BlockSpec((CONV_WIDTH, tc), lambda i, j: (0, j))],
        out_specs=pl.BlockSpec((tr, tc), lambda i, j: (i, j)),
        out_shape=jax.ShapeDtypeStruct((s, width), F32),
        scratch_shapes=[pltpu.VMEM((tr + 2 * halo, tc), F32)],
        compiler_params=_params("parallel", "parallel"),
        name="gdn_conv",
    )(p, p, p, conv_w)


def _unit_triangular_inverse(lm):
    c = lm.shape[0]
    i = lax.broadcasted_iota(jnp.int32, (c, c), 0)
    j = lax.broadcasted_iota(jnp.int32, (c, c), 1)
    inv = jnp.where(i == j, 1.0, 0.0) - lm
    power = _dot_f32(lm, lm)
    span = 2
    while span < c:
        inv = inv + _dot_f32(inv, power)
        span *= 2
        if span < c:
            power = _dot_f32(power, power)
    return inv


def _gdn_prep_kernel(q_ref, k_ref, v_ref, beta_ref, cum_ref, u_ref, w_ref, attn_ref, qg_ref, kd_ref):
    d = pl.program_id(0)
    c = q_ref.shape[0]
    q = q_ref[...]
    k = k_ref[...]
    v = v_ref[...]
    cum_row = cum_ref[...]
    cum = _row_to_col(cum_row)
    beta = _row_to_col(beta_ref[...])
    incl, strict = _dir_masks(c, d)
    diff = cum - cum_row
    decay = jnp.where(incl, jnp.exp(jnp.where(incl, diff, 0.0)), 0.0)
    k_beta = k * beta
    kb = k.astype(BF16)
    lower = _dot_nt(k_beta.astype(BF16), kb) * jnp.where(strict, decay, 0.0)
    inv = _unit_triangular_inverse(lower)
    e_cum = jnp.exp(cum)
    u_ref[...] = _dot_f32(inv, v * beta)
    w_ref[...] = _dot_f32(inv, k_beta * e_cum).astype(w_ref.dtype)
    attn_ref[...] = (_dot_nt(q.astype(BF16), kb) * decay).astype(attn_ref.dtype)
    qg_ref[...] = (q * e_cum).astype(qg_ref.dtype)
    last = _lane_pick(cum_row, jnp.where(d == 0, c - 1, 0))
    kd_ref[...] = (k * jnp.exp(last - cum)).astype(kd_ref.dtype)


def _gdn_scan_kernel(u_ref, w_ref, attn_ref, qg_ref, kd_ref, cum_ref, o_ref, state_ref):
    d = pl.program_id(0)

    @pl.when(pl.program_id(2) == 0)
    def _():
        state_ref[...] = jnp.zeros_like(state_ref)

    c = u_ref.shape[0]
    state = state_ref[...]
    sb = state.astype(BF16)
    v_new = (u_ref[...] - _dot(w_ref[...], sb)).astype(BF16)
    o_ref[...] = _dot(attn_ref[...], v_new) + _dot(qg_ref[...], sb)
    last = _lane_pick(cum_ref[...], jnp.where(d == 0, c - 1, 0))
    state_ref[...] = state * jnp.exp(last) + _dot_tn(kd_ref[...], v_new)


def _gdn(qkv, gates3, n_heads, beta_row0, cum_row0):
    s = qkv.shape[0]
    dk = GDN_HEAD_DIM
    nc = s // CHUNK
    width = n_heads * dk

    def col(o):
        return pl.BlockSpec((CHUNK, dk), lambda d, h, c: (c, o + h))

    def gate_row(r0):
        return pl.BlockSpec((None, 1, CHUNK), lambda d, h, c: (r0 + d * n_heads + h, 0, c))

    out_blk = pl.BlockSpec((None, CHUNK, dk), lambda d, h, c: (d, c, h))
    u, w, attn, qg, kd = pl.pallas_call(
        _gdn_prep_kernel,
        grid=(2, n_heads, nc),
        in_specs=[col(0), col(n_heads), col(2 * n_heads), gate_row(beta_row0), gate_row(cum_row0)],
        out_specs=[out_blk] * 5,
        out_shape=[jax.ShapeDtypeStruct((2, s, width), dt) for dt in (F32, BF16, BF16, BF16, BF16)],
        compiler_params=_params("parallel", "parallel", "parallel"),
        name="gdn_prep",
    )(qkv, qkv, qkv, gates3, gates3)

    seq_blk = pl.BlockSpec((None, CHUNK, dk), lambda d, h, c: (d, _chunk_index(d, c, nc), h))
    cum_blk = pl.BlockSpec((None, 1, CHUNK),
                           lambda d, h, c: (cum_row0 + d * n_heads + h, 0, _chunk_index(d, c, nc)))
    return pl.pallas_call(
        _gdn_scan_kernel,
        grid=(2, n_heads, nc),
        in_specs=[seq_blk] * 5 + [cum_blk],
        out_specs=seq_blk,
        out_shape=jax.ShapeDtypeStruct((2, s, width), F32),
        scratch_shapes=[pltpu.VMEM((dk, dk), F32)],
        compiler_params=_params("parallel", "parallel", "arbitrary"),
        name="gdn_scan",
    )(u, w, attn, qg, kd, gates3)


def _mlstm_kernel(q_ref, k_ref, v_ref, ig_ref, cum_ref, o_ref, mem_ref, nrm_ref, m_ref):
    d = pl.program_id(0)

    @pl.when(pl.program_id(2) == 0)
    def _():
        mem_ref[...] = jnp.zeros_like(mem_ref)
        nrm_ref[...] = jnp.zeros_like(nrm_ref)
        m_ref[...] = jnp.zeros_like(m_ref)

    c, dk = q_ref.shape
    q = q_ref[...]
    k = k_ref[...] * (dk ** -0.5)
    vb = v_ref[...].astype(BF16)
    qb = q.astype(BF16)
    ig_row = ig_ref[...]
    cum_row = cum_ref[...]
    ig = _row_to_col(ig_row)
    cum = _row_to_col(cum_row)
    m_prev = m_ref[0:1, 0:1]
    mem = mem_ref[...]
    nrm = nrm_ref[...]

    incl, _ = _dir_masks(c, d)
    log_d = jnp.where(incl, cum - cum_row + ig_row, -jnp.inf)
    m_inter = cum + m_prev
    m_row = jnp.maximum(m_inter, jnp.max(log_d, axis=1, keepdims=True))
    s_mat = _dot_nt(qb, k.astype(BF16)) * jnp.exp(log_d - m_row)
    inter_scale = jnp.exp(m_inter - m_row)
    num = _dot(s_mat.astype(BF16), vb) + inter_scale * _dot(qb, mem.astype(BF16))
    den = (jnp.sum(s_mat, axis=1, keepdims=True)
           + inter_scale * jnp.sum(q * nrm, axis=1, keepdims=True))
    o_ref[...] = num / jnp.maximum(jnp.abs(den), jnp.exp(-m_row))

    b_last = _lane_pick(cum_row, jnp.where(d == 0, c - 1, 0))
    log_w = b_last - cum + ig
    m_new = jnp.maximum(b_last + m_prev, jnp.max(log_w, axis=0, keepdims=True))
    kw = k * jnp.exp(log_w - m_new)
    carry_scale = jnp.exp(b_last + m_prev - m_new)
    mem_ref[...] = carry_scale * mem + _dot_tn(kw.astype(BF16), vb)
    nrm_ref[...] = carry_scale * nrm + jnp.sum(kw, axis=0, keepdims=True)
    m_ref[...] = jnp.broadcast_to(m_new, m_ref.shape)


def _mlstm(p, offs, gates3, n_heads, ig_row0, cum_row0):
    s = p.shape[0]
    dk = MLSTM_HEAD_DIM
    nc = s // CHUNK
    qo, ko, vo = (o // dk for o in offs)

    def col(o):
        return pl.BlockSpec((CHUNK, dk), lambda d, h, c: (_chunk_index(d, c, nc), o + h))

    def gate_row(r0):
        return pl.BlockSpec((None, 1, CHUNK),
                            lambda d, h, c: (r0 + d * n_heads + h, 0, _chunk_index(d, c, nc)))

    return pl.pallas_call(
        _mlstm_kernel,
        grid=(2, n_heads, nc),
        in_specs=[col(qo), col(ko), col(vo), gate_row(ig_row0), gate_row(cum_row0)],
        out_specs=pl.BlockSpec((None, CHUNK, dk), lambda d, h, c: (d, _chunk_index(d, c, nc), h)),
        out_shape=jax.ShapeDtypeStruct((2, s, n_heads * dk), F32),
        scratch_shapes=[pltpu.VMEM((dk, dk), F32), pltpu.VMEM((1, dk), F32), pltpu.VMEM((8, 128), F32)],
        compiler_params=_params("parallel", "parallel", "arbitrary"),
        name="mlstm_scan",
    )(p, p, p, gates3, gates3)


def _finalize_kernel(y_ref, g_ref, w_ref, o_ref, *, head_dim, subtract_mean, swish_gate):
    y = y_ref[0] + y_ref[1]
    g = g_ref[...]
    gate = jax.nn.sigmoid(g)
    if swish_gate:
        gate = g * gate
    w = w_ref[...]
    for hh in range(y.shape[1] // head_dim):
        sl = slice(hh * head_dim, (hh + 1) * head_dim)
        t = y[:, sl]
        if subtract_mean:
            t = t - jnp.mean(t, axis=-1, keepdims=True)
        t = t * lax.rsqrt(jnp.mean(t * t, axis=-1, keepdims=True) + EPS)
        o_ref[:, sl] = (t * w[:, sl] * gate[:, sl]).astype(o_ref.dtype)


def _finalize(y2, p, gate_off, norm_w, head_dim, subtract_mean, swish_gate):
    _, s, width = y2.shape
    tm = _pick(s, (256, 128))
    tc = _pick(width, (512, 256))
    go = gate_off // tc
    return pl.pallas_call(
        functools.partial(_finalize_kernel, head_dim=head_dim, subtract_mean=subtract_mean,
                          swish_gate=swish_gate),
        grid=(s // tm, width // tc),
        in_specs=[pl.BlockSpec((2, tm, tc), lambda i, j: (0, i, j)),
                  pl.BlockSpec((tm, tc), lambda i, j: (i, go + j)),
                  pl.BlockSpec((1, tc), lambda i, j: (0, j))],
        out_specs=pl.BlockSpec((tm, tc), lambda i, j: (i, j)),
        out_shape=jax.ShapeDtypeStruct((s, width), BF16),
        compiler_params=_params("parallel", "parallel"),
        name="branch_finalize",
    )(y2, p, norm_w.reshape(1, width))


def _merge_kernel(ya_ref, yb_ref, yc_ref, wb_ref, la_ref, lb_ref, lc_ref, o_ref):
    acc = jax.nn.sigmoid(la_ref[...]) * _dot(ya_ref[...], wb_ref[0])
    acc = acc + jax.nn.sigmoid(lb_ref[...]) * _dot(yb_ref[...], wb_ref[1])
    acc = acc + jax.nn.sigmoid(lc_ref[...]) * _dot(yc_ref[...], wb_ref[2])
    o_ref[...] = acc.astype(o_ref.dtype)


def _merge(ys, w_branch, p, logit_off, d_model):
    s, width = ys[0].shape
    tm = _pick(s, (512, 256, 128))
    tn = _pick(d_model, (512, 256, 128))
    lo = logit_off // tn
    nb = d_model // tn
    y_blk = pl.BlockSpec((tm, width), lambda i, j: (i, 0))

    def logit(r):
        return pl.BlockSpec((tm, tn), lambda i, j: (i, lo + r * nb + j))

    return pl.pallas_call(
        _merge_kernel,
        grid=(s // tm, nb),
        in_specs=[y_blk, y_blk, y_blk, pl.BlockSpec((N_BRANCHES, width, tn), lambda i, j: (0, 0, j)),
                  logit(0), logit(1), logit(2)],
        out_specs=pl.BlockSpec((tm, tn), lambda i, j: (i, j)),
        out_shape=jax.ShapeDtypeStruct((s, d_model), BF16),
        compiler_params=_params("parallel", "parallel"),
        name="merge",
    )(*ys, w_branch, p, p, p)


def kernel(x, positions, norm_mix, w_in, conv_w, gdn_a_log, gdn_dt_bias, mlstm_i_bias, mlstm_f_bias,
           ret_norm, gdn_norm, mlstm_norm, w_branch, w_out, norm_ffn, w_ffn_in, w_ffn_out, norm_final):
    batch, s, d = x.shape
    assert batch == 1 and s % CHUNK == 0
    depth = w_in.shape[0]
    mix = d // 2
    ret_heads = mix // RET_HEAD_DIM
    gdn_heads = mix // GDN_HEAD_DIM
    mlstm_heads = mix // MLSTM_HEAD_DIM
    d_ff = w_ffn_out.shape[1]
    assert 4 * gdn_heads + 4 * mlstm_heads <= GATE_ROWS

    sizes = (mix, mix, mix, mix, 3 * mix, mix, 2 * gdn_heads, 2 * gdn_heads,
             mix, mix, mix, mix, 2 * mlstm_heads, 2 * mlstm_heads, N_BRANCHES * d)
    starts = np.concatenate([[0], np.cumsum(sizes)]).tolist()
    wide = [idx for idx, sz in enumerate(sizes) if sz >= 128]
    small = [idx for idx, sz in enumerate(sizes) if sz < 128]
    off, acc = {}, 0
    for idx in wide:
        off[idx] = acc
        acc += sizes[idx]

    x2 = x.reshape(s, d)
    tables = _rope_table(positions.reshape(s), RET_HEAD_DIM // 2)
    n_small = sum(sizes[idx] for idx in small)

    for l in range(depth):
        w_l = w_in[l]
        w_main = jnp.concatenate([w_l[:, starts[idx]:starts[idx + 1]] for idx in wide], axis=1).astype(BF16)
        w_small_t = jnp.concatenate([w_l[:, starts[idx]:starts[idx + 1]] for idx in small], axis=1).T
        w_small_t = jnp.pad(w_small_t, ((0, GATE_ROWS - n_small), (0, 0))).astype(BF16)
        zeros = jnp.zeros((2 * gdn_heads,), F32)
        gate_bias = jnp.concatenate([zeros, gdn_dt_bias[l].reshape(-1), mlstm_i_bias[l].reshape(-1),
                                     mlstm_f_bias[l].reshape(-1)])
        gate_scale = jnp.concatenate([zeros + 1.0, -jnp.exp(gdn_a_log[l].reshape(-1)),
                                      jnp.ones((4 * mlstm_heads,), F32)])
        gate_par = jnp.pad(jnp.stack([gate_bias, gate_scale], axis=1), ((0, GATE_ROWS - n_small), (0, 0)))

        h = _rmsnorm(x2, norm_mix[l], BF16)
        p = _matmul_w_resident(h, w_main, F32)
        gates3 = _gates(h, w_small_t, gate_par, gdn_heads, mlstm_heads).reshape(GATE_ROWS, 1, s)

        y_ret = _retention(p, (off[0], off[1], off[2]), tables, ret_heads)
        y_ret = _finalize(y_ret, p, off[3], ret_norm[l], RET_HEAD_DIM, True, True)

        qkv = _gdn_conv(p, off[4], 3 * mix, conv_w[l])
        y_gdn = _gdn(qkv, gates3, gdn_heads, 0, 2 * gdn_heads)
        y_gdn = _finalize(y_gdn, p, off[5], gdn_norm[l], GDN_HEAD_DIM, False, True)

        y_ml = _mlstm(p, (off[8], off[9], off[10]), gates3, mlstm_heads,
                      4 * gdn_heads, 4 * gdn_heads + 2 * mlstm_heads)
        y_ml = _finalize(y_ml, p, off[11], mlstm_norm[l], MLSTM_HEAD_DIM, False, False)

        merged = _merge((y_ret, y_gdn, y_ml), w_branch[l].astype(BF16), p, off[14], d)
        x2 = _matmul_residual(merged, w_out[l].astype(BF16), x2, "out_proj")

        h = _rmsnorm(x2, norm_ffn[l], BF16)
        act = _ffn_in(h, w_ffn_in[l].astype(BF16), d_ff)
        x2 = _matmul_residual(act, w_ffn_out[l].astype(BF16), x2, "ffn_out")

    return _rmsnorm(x2, norm_final, x.dtype).reshape(batch, s, d)
```

```python
import functools

import numpy as np
import jax
import jax.numpy as jnp
from jax import lax
from jax.experimental import pallas as pl
from jax.experimental.pallas import tpu as pltpu

RET_HEAD_DIM = 256
GDN_HEAD_DIM = 128
MLSTM_HEAD_DIM = 256
CONV_WIDTH = 5
ROPE_BASE = 10000.0
N_BRANCHES = 3
EPS = 1e-6

CHUNK = 128
GATE_ROWS = 128
F32_SUBLANES = 8
V7X_VMEM_BYTES = 64 * 1024 * 1024
VMEM_LIMIT = V7X_VMEM_BYTES - 8 * 1024 * 1024

F32 = jnp.float32
BF16 = jnp.bfloat16


def _params(*sem):
    return pltpu.CompilerParams(dimension_semantics=sem, vmem_limit_bytes=VMEM_LIMIT)


def _pick(n, candidates):
    for c in candidates:
        if n % c == 0:
            return c
    raise ValueError(f"no tile in {candidates} divides {n}")


def _dot(a, b):
    return jnp.dot(a, b, preferred_element_type=F32)


def _dot_nt(a, b):
    return lax.dot_general(a, b, (((1,), (1,)), ((), ())), preferred_element_type=F32)


def _dot_tn(a, b):
    return lax.dot_general(a, b, (((0,), (0,)), ((), ())), preferred_element_type=F32)


def _dot_f32(a, b):
    return jnp.dot(a, b, preferred_element_type=F32, precision=lax.Precision.HIGHEST)


def _split(a):
    hi = a.astype(BF16)
    return hi, (a - hi.astype(F32)).astype(BF16)


def _dot_split(a, b):
    a_hi, a_lo = _split(a)
    b_hi, b_lo = _split(b)
    return _dot(a_hi, b_hi) + (_dot(a_hi, b_lo) + _dot(a_lo, b_hi))


def _row_to_col(row):
    c = row.shape[-1]
    i = lax.broadcasted_iota(jnp.int32, (c, c), 0)
    j = lax.broadcasted_iota(jnp.int32, (c, c), 1)
    return jnp.sum(jnp.where(i == j, jnp.broadcast_to(row, (c, c)), 0.0), axis=1, keepdims=True)


def _lane_pick(row, idx):
    lane = lax.broadcasted_iota(jnp.int32, row.shape, 1)
    return jnp.sum(jnp.where(lane == idx, row, 0.0), axis=1, keepdims=True)


def _dir_masks(c, direction):
    i = lax.broadcasted_iota(jnp.int32, (c, c), 0)
    j = lax.broadcasted_iota(jnp.int32, (c, c), 1)
    if direction == 0:
        return j <= i, j < i
    return j >= i, j > i


def _rmsnorm_kernel(x_ref, w_ref, o_ref):
    x = x_ref[...]
    y = x * lax.rsqrt(jnp.mean(x * x, axis=-1, keepdims=True) + EPS)
    o_ref[...] = (y * w_ref[...]).astype(o_ref.dtype)


def _rmsnorm(x, w, out_dtype):
    s, d = x.shape
    tm = _pick(s, (256, 128))
    return pl.pallas_call(
        _rmsnorm_kernel,
        grid=(s // tm,),
        in_specs=[pl.BlockSpec((tm, d), lambda i: (i, 0)), pl.BlockSpec((1, d), lambda i: (0, 0))],
        out_specs=pl.BlockSpec((tm, d), lambda i: (i, 0)),
        out_shape=jax.ShapeDtypeStruct((s, d), out_dtype),
        compiler_params=_params("parallel"),
        name="rmsnorm",
    )(x, w.reshape(1, d))


def _mm_kernel(a_ref, b_ref, o_ref):
    o_ref[...] = _dot(a_ref[...], b_ref[...]).astype(o_ref.dtype)


def _mm_cast_kernel(a_ref, w_ref, o_ref, wb_ref):
    @pl.when(pl.program_id(1) == 0)
    def _():
        wb_ref[...] = w_ref[...].astype(BF16)

    o_ref[...] = _dot(a_ref[...], wb_ref[...]).astype(o_ref.dtype)


def _mm_cast_res_kernel(a_ref, w_ref, r_ref, o_ref, wb_ref):
    @pl.when(pl.program_id(1) == 0)
    def _():
        wb_ref[...] = w_ref[...].astype(BF16)

    o_ref[...] = r_ref[...] + _dot(a_ref[...], wb_ref[...])


def _matmul_bf16w(a, b, out_dtype, name):
    m, k = a.shape
    n = b.shape[1]
    tm = _pick(m, (1024, 512, 256, 128))
    tn = _pick(n, (1024, 512, 256, 128))
    return pl.pallas_call(
        _mm_kernel,
        grid=(n // tn, m // tm),
        in_specs=[pl.BlockSpec((tm, k), lambda j, i: (i, 0)), pl.BlockSpec((k, tn), lambda j, i: (0, j))],
        out_specs=pl.BlockSpec((tm, tn), lambda j, i: (i, j)),
        out_shape=jax.ShapeDtypeStruct((m, n), out_dtype),
        compiler_params=_params("parallel", "parallel"),
        name=name,
    )(a, b)


def _matmul_f32w(a, w_stack, layer, n, name, res=None):
    m, k = a.shape
    tm = _pick(m, (1024, 512, 256, 128))
    tn = _pick(n, (512, 256, 128))
    in_specs = [pl.BlockSpec((tm, k), lambda j, i: (i, 0)),
                pl.BlockSpec((None, k, tn), lambda j, i: (layer, 0, j))]
    operands = [a, w_stack]
    if res is not None:
        in_specs.append(pl.BlockSpec((tm, tn), lambda j, i: (i, j)))
        operands.append(res)
    return pl.pallas_call(
        _mm_cast_kernel if res is None else _mm_cast_res_kernel,
        grid=(n // tn, m // tm),
        in_specs=in_specs,
        out_specs=pl.BlockSpec((tm, tn), lambda j, i: (i, j)),
        out_shape=jax.ShapeDtypeStruct((m, n), F32),
        scratch_shapes=[pltpu.VMEM((k, tn), BF16)],
        compiler_params=_params("parallel", "arbitrary"),
        name=name,
    )(*operands)


def _mm_wt_kernel(a_ref, wt_ref, o_ref, wb_ref):
    @pl.when(pl.program_id(1) == 0)
    def _():
        wb_ref[...] = wt_ref[0].astype(BF16)

    o_ref[...] = _dot_nt(a_ref[...], wb_ref[...]).astype(o_ref.dtype)


def _matmul_wt(a, wt_stack, layer, row0, n, name):
    m, k = a.shape
    tm = _pick(m, (1024, 512, 256, 128))
    tn = _pick(n, (512, 256, 128))
    return pl.pallas_call(
        _mm_wt_kernel,
        grid=(n // tn, m // tm),
        in_specs=[pl.BlockSpec((tm, k), lambda j, i: (i, 0)),
                  pl.BlockSpec((pl.Element(1), pl.Element(tn), pl.Element(k)),
                               lambda j, i: (layer, pl.multiple_of(row0 + j * tn, F32_SUBLANES), 0))],
        out_specs=pl.BlockSpec((tm, tn), lambda j, i: (i, j)),
        out_shape=jax.ShapeDtypeStruct((m, n), F32),
        scratch_shapes=[pltpu.VMEM((tn, k), BF16)],
        compiler_params=_params("parallel", "arbitrary"),
        name=name,
    )(a, wt_stack)


def _mm_res_kernel(a_ref, b_ref, r_ref, o_ref):
    o_ref[...] = r_ref[...] + _dot(a_ref[...], b_ref[...])


def _matmul_residual(a, b, res, name):
    m, k = a.shape
    n = b.shape[1]
    tm = _pick(m, (1024, 512, 256, 128))
    tn = _pick(n, (256, 128))
    return pl.pallas_call(
        _mm_res_kernel,
        grid=(m // tm, n // tn),
        in_specs=[pl.BlockSpec((tm, k), lambda i, j: (i, 0), pipeline_mode=pl.Buffered(1)),
                  pl.BlockSpec((k, tn), lambda i, j: (0, j)),
                  pl.BlockSpec((tm, tn), lambda i, j: (i, j))],
        out_specs=pl.BlockSpec((tm, tn), lambda i, j: (i, j)),
        out_shape=jax.ShapeDtypeStruct((m, n), F32),
        compiler_params=_params("parallel", "arbitrary"),
        name=name,
    )(a, b, res)


def _swiglu_kernel(a_ref, wg_ref, wu_ref, o_ref, wgb_ref, wub_ref):
    @pl.when(pl.program_id(1) == 0)
    def _():
        wgb_ref[...] = wg_ref[...].astype(BF16)
        wub_ref[...] = wu_ref[...].astype(BF16)

    a = a_ref[...]
    gate = _dot(a, wgb_ref[...])
    up = _dot(a, wub_ref[...])
    o_ref[...] = (gate * jax.nn.sigmoid(gate) * up).astype(o_ref.dtype)


def _ffn_in(h, w_stack, layer, d_ff):
    m, k = h.shape
    tm = _pick(m, (1024, 512, 256, 128))
    tn = _pick(d_ff, (256, 128))
    nb = d_ff // tn
    return pl.pallas_call(
        _swiglu_kernel,
        grid=(nb, m // tm),
        in_specs=[pl.BlockSpec((tm, k), lambda j, i: (i, 0)),
                  pl.BlockSpec((None, k, tn), lambda j, i: (layer, 0, j)),
                  pl.BlockSpec((None, k, tn), lambda j, i: (layer, 0, j + nb))],
        out_specs=pl.BlockSpec((tm, tn), lambda j, i: (i, j)),
        out_shape=jax.ShapeDtypeStruct((m, d_ff), BF16),
        scratch_shapes=[pltpu.VMEM((k, tn), BF16), pltpu.VMEM((k, tn), BF16)],
        compiler_params=_params("parallel", "arbitrary"),
        name="ffn_in",
    )(h, w_stack, w_stack)


def _gates_kernel(w_ref, h_ref, par_ref, o_ref, *, n_gdn, n_mlstm):
    z = _dot_nt(w_ref[...], h_ref[...])
    r, tm = z.shape
    bias = par_ref[:, 0:1]
    scale = par_ref[:, 1:2]
    row = lax.broadcasted_iota(jnp.int32, (r, tm), 0)
    zb = z + bias
    soft = jnp.log1p(jnp.exp(-jnp.abs(zb)))
    softplus = jnp.maximum(zb, 0.0) + soft
    log_sig = jnp.minimum(zb, 0.0) - soft
    b0, b1, b2, b3 = 2 * n_gdn, 4 * n_gdn, 4 * n_gdn + 2 * n_mlstm, 4 * n_gdn + 4 * n_mlstm
    is_alpha = (row >= b0) & (row < b1)
    is_logf = (row >= b2) & (row < b3)
    raw = jnp.where(row < b0, jax.nn.sigmoid(zb),
                    jnp.where(is_alpha, scale * softplus,
                              jnp.where(row < b2, zb, jnp.where(is_logf, log_sig, 0.0))))
    pj = lax.broadcasted_iota(jnp.int32, (tm, tm), 0)
    pi = lax.broadcasted_iota(jnp.int32, (tm, tm), 1)
    same = (pj // CHUNK) == (pi // CHUNK)
    prefix = jnp.where(same & (pj <= pi), 1.0, 0.0)
    suffix = jnp.where(same & (pj >= pi), 1.0, 0.0)
    cum_p = _dot_f32(raw, prefix)
    cum_s = _dot_f32(raw, suffix)
    dir1 = (is_alpha & (row >= b0 + n_gdn)) | (is_logf & (row >= b2 + n_mlstm))
    cum = jnp.where(dir1, cum_s, cum_p)
    o_ref[...] = jnp.where(is_alpha | is_logf, cum, raw)


def _gates(h, w, par, n_gdn, n_mlstm):
    s, d = h.shape
    tm = _pick(s, (256, 128))
    return pl.pallas_call(
        functools.partial(_gates_kernel, n_gdn=n_gdn, n_mlstm=n_mlstm),
        grid=(s // tm,),
        in_specs=[pl.BlockSpec((GATE_ROWS, d), lambda i: (0, 0)),
                  pl.BlockSpec((tm, d), lambda i: (i, 0)),
                  pl.BlockSpec((GATE_ROWS, 2), lambda i: (0, 0))],
        out_specs=pl.BlockSpec((GATE_ROWS, tm), lambda i: (0, i)),
        out_shape=jax.ShapeDtypeStruct((GATE_ROWS, s), F32),
        compiler_params=_params("parallel"),
        name="gates",
    )(w, h, par)


def _rope_kernel(pos_ref, freq_ref, cos_ref, sin_ref):
    ang = pos_ref[...].astype(F32) * freq_ref[...]
    cos_ref[...] = jnp.cos(ang)
    sin_ref[...] = jnp.sin(ang)


def _rope_table(positions, half):
    s = positions.shape[0]
    tm = _pick(s, (512, 256, 128))
    inv_freq = (ROPE_BASE ** (-jnp.arange(half, dtype=F32) / half)).reshape(1, half)
    return pl.pallas_call(
        _rope_kernel,
        grid=(s // tm,),
        in_specs=[pl.BlockSpec((tm, 1), lambda i: (i, 0)), pl.BlockSpec((1, half), lambda i: (0, 0))],
        out_specs=[pl.BlockSpec((tm, half), lambda i: (i, 0))] * 2,
        out_shape=[jax.ShapeDtypeStruct((s, half), F32)] * 2,
        compiler_params=_params("parallel"),
        name="rope_table",
    )(positions.reshape(s, 1), inv_freq)


def _both_directions(make_spec, nc):
    return [make_spec(lambda c: c), make_spec(lambda c: nc - 1 - c)]


def _retention_kernel(lg_ref, qf_ref, kf_ref, vf_ref, cosf_ref, sinf_ref,
                      qb_ref, kb_ref, vb_ref, cosb_ref, sinb_ref, of_ref, ob_ref, state_ref, *, heads):
    hg = pl.program_id(0)

    @pl.when(pl.program_id(1) == 0)
    def _():
        state_ref[...] = jnp.zeros_like(state_ref)

    c = qf_ref.shape[0]
    dk = RET_HEAD_DIM
    half = dk // 2
    i = lax.broadcasted_iota(jnp.int32, (c, c), 0)
    j = lax.broadcasted_iota(jnp.int32, (c, c), 1)
    dist = jnp.abs(i - j).astype(F32)
    idx = lax.broadcasted_iota(jnp.int32, (c, 1), 0).astype(F32)
    refs = ((qf_ref, kf_ref, vf_ref, cosf_ref, sinf_ref, of_ref),
            (qb_ref, kb_ref, vb_ref, cosb_ref, sinb_ref, ob_ref))
    chains = []
    for direction, (q_ref, k_ref, v_ref, cos_ref, sin_ref, o_ref) in enumerate(refs):
        cos = cos_ref[...]
        sin = sin_ref[...]

        def rot(t, cos=cos, sin=sin):
            t1, t2 = t[:, :half], t[:, half:]
            return jnp.concatenate([t1 * cos - t2 * sin, t1 * sin + t2 * cos], axis=1)

        mask = (j <= i) if direction == 0 else (j > i)
        for hh in range(heads):
            sl = slice(hh * dk, (hh + 1) * dk)
            lg = jnp.full((1, 1), lg_ref[hg * heads + hh], F32)
            decay_intra = jnp.where(mask, jnp.exp(lg * jnp.where(mask, dist, 0.0)), 0.0)
            q_decay = jnp.exp(lg * ((idx + 1.0) if direction == 0 else (c - idx)))
            k_decay = jnp.exp(lg * ((c - 1.0 - idx) if direction == 0 else idx))
            q = rot(q_ref[:, sl])
            k = rot(k_ref[:, sl]) * (dk ** -0.5)
            v = v_ref[:, sl].astype(BF16)
            scores = (_dot_nt(q.astype(BF16), k.astype(BF16)) * decay_intra).astype(BF16)
            chains.append((direction, hh, sl, o_ref, scores, v, (q * q_decay).astype(BF16),
                           (k * k_decay).astype(BF16), jnp.exp(lg * c)))
    states = [state_ref[direction, hh] for direction, hh, *_ in chains]
    for (direction, hh, sl, o_ref, scores, v, qd, kd, _), state in zip(chains, states):
        o_ref[:, sl] = _dot(scores, v) + _dot(qd, state.astype(BF16))
    for (direction, hh, sl, o_ref, scores, v, qd, kd, chunk_decay), state in zip(chains, states):
        state_ref[direction, hh] = state * chunk_decay + _dot_tn(kd, v)


def _retention(p, offs, tables, n_heads):
    s = p.shape[0]
    dk = RET_HEAD_DIM
    nc = s // CHUNK
    heads = 2 if n_heads % 2 == 0 else 1
    wb = heads * dk
    cos, sin = tables
    qo, ko, vo = (o // wb for o in offs)
    log_gamma = jnp.log1p(-jnp.exp2(-5.0 - jnp.arange(n_heads, dtype=F32)))

    def specs(chunk):
        col = [pl.BlockSpec((CHUNK, wb), lambda h, c, o=o: (chunk(c), o + h)) for o in (qo, ko, vo)]
        tab = pl.BlockSpec((CHUNK, dk // 2), lambda h, c: (chunk(c), 0))
        return col + [tab, tab]

    fwd, bwd = _both_directions(specs, nc)
    out_f, out_b = _both_directions(lambda chunk: pl.BlockSpec((CHUNK, wb), lambda h, c: (chunk(c), h)), nc)
    return pl.pallas_call(
        functools.partial(_retention_kernel, heads=heads),
        grid=(n_heads // heads, nc),
        in_specs=[pl.BlockSpec(memory_space=pltpu.SMEM)] + fwd + bwd,
        out_specs=[out_f, out_b],
        out_shape=[jax.ShapeDtypeStruct((s, n_heads * dk), F32)] * 2,
        scratch_shapes=[pltpu.VMEM((2, heads, dk, dk), F32)],
        compiler_params=_params("parallel", "arbitrary"),
        name="retention_scan",
    )(log_gamma, p, p, p, cos, sin, p, p, p, cos, sin)


def _conv_kernel(prev_ref, cur_ref, next_ref, w_ref, o_ref, pad_ref, *, n_qk_blocks, n_q_blocks, halo):
    i = pl.program_id(0)
    jb = pl.program_id(1)
    tr, tc = cur_ref.shape
    keep_prev = jnp.where(i == 0, 0.0, 1.0)
    keep_next = jnp.where(i == pl.num_programs(0) - 1, 0.0, 1.0)
    pad_ref[0:halo, :] = prev_ref[...] * keep_prev
    pad_ref[halo:halo + tr, :] = cur_ref[...]
    pad_ref[halo + tr:, :] = next_ref[...] * keep_next
    side = (CONV_WIDTH - 1) // 2
    acc = jnp.zeros((tr, tc), F32)
    for tap in range(CONV_WIDTH):
        acc = acc + pad_ref[pl.ds(halo - side + tap, tr), :] * w_ref[tap:tap + 1, :]
    y = acc * jax.nn.sigmoid(acc)
    is_qk = jb < n_qk_blocks
    q_scale = jnp.where(jb < n_q_blocks, GDN_HEAD_DIM ** -0.5, 1.0)
    for g in range(tc // GDN_HEAD_DIM):
        sl = slice(g * GDN_HEAD_DIM, (g + 1) * GDN_HEAD_DIM)
        t = y[:, sl]
        normed = t * lax.rsqrt(jnp.sum(t * t, axis=-1, keepdims=True) + EPS) * q_scale
        o_ref[:, sl] = jnp.where(is_qk, normed, t)


def _gdn_conv(p, off, width, conv_w):
    s = p.shape[0]
    halo = 8
    tr = _pick(s, (512, 256, 128))
    tc = _pick(width // 3, (512, 256, 128))
    ob = off // tc
    n_row8 = s // halo
    return pl.pallas_call(
        functools.partial(_conv_kernel, n_qk_blocks=2 * (width // 3) // tc, n_q_blocks=(width // 3) // tc,
                          halo=halo),
        grid=(s // tr, width // tc),
        in_specs=[pl.BlockSpec((halo, tc), lambda i, j: (jnp.maximum(i * (tr // halo) - 1, 0), ob + j)),
                  pl.BlockSpec((tr, tc), lambda i, j: (i, ob + j)),
                  pl.BlockSpec((halo, tc), lambda i, j: (jnp.minimum((i + 1) * (tr // halo), n_row8 - 1), ob + j)),
                  pl.BlockSpec((CONV_WIDTH, tc), lambda i, j: (0, j))],
        out_specs=pl.BlockSpec((tr, tc), lambda i, j: (i, j)),
        out_shape=jax.ShapeDtypeStruct((s, width), F32),
        scratch_shapes=[pltpu.VMEM((tr + 2 * halo, tc), F32)],
        compiler_params=_params("parallel", "parallel"),
        name="gdn_conv",
    )(p, p, p, conv_w)


def _unit_triangular_inverses(lms):
    c = lms[0].shape[0]
    i = lax.broadcasted_iota(jnp.int32, (c, c), 0)
    j = lax.broadcasted_iota(jnp.int32, (c, c), 1)
    level = jnp.bitwise_xor(i, j)
    eye = jnp.where(i == j, 1.0, 0.0)
    xs = [eye - jnp.where(level == 1, lm, 0.0) for lm in lms]
    span = 2
    while span < c:
        joins = (level >= span) & (level < 2 * span)
        ys = [_dot(jnp.where(joins, lm, 0.0).astype(BF16), x.astype(BF16)) for lm, x in zip(lms, xs)]
        xs = [x - _dot(x.astype(BF16), y.astype(BF16)) for x, y in zip(xs, ys)]
        span *= 2
    resids = [eye - x - _dot_split(lm, x) for lm, x in zip(lms, xs)]
    return [x + _dot(x.astype(BF16), r.astype(BF16)) for x, r in zip(xs, resids)]


def _gdn_prep_kernel(q_ref, k_ref, v_ref, beta_ref, cum_ref, u_ref, w_ref, attn_ref, qg_ref, kd_ref,
                     *, heads):
    c = q_ref.shape[0]
    dk = GDN_HEAD_DIM
    problems = [(direction, hh) for direction in range(2) for hh in range(heads)]
    masks = [_dir_masks(c, direction) for direction in range(2)]
    ks = [k_ref[:, hh * dk:(hh + 1) * dk] for hh in range(heads)]
    kbs = [k.astype(BF16) for k in ks]
    cum_rows = [cum_ref[direction, hh] for direction, hh in problems]
    cums = [_row_to_col(row) for row in cum_rows]
    betas = [_row_to_col(beta_ref[direction, hh]) for direction, hh in problems]
    decays = []
    for (direction, _), cum, row in zip(problems, cums, cum_rows):
        incl = masks[direction][0]
        decays.append(jnp.where(incl, jnp.exp(jnp.where(incl, cum - row, 0.0)), 0.0))
    k_betas = [ks[hh] * beta for (_, hh), beta in zip(problems, betas)]
    lowers = [_dot_nt(k_beta.astype(BF16), kbs[hh]) * jnp.where(masks[direction][1], decay, 0.0)
              for (direction, hh), k_beta, decay in zip(problems, k_betas, decays)]
    invs = _unit_triangular_inverses(lowers)
    e_cums = [jnp.exp(cum) for cum in cums]
    us = [_dot_split(inv, v_ref[:, hh * dk:(hh + 1) * dk] * beta)
          for (_, hh), inv, beta in zip(problems, invs, betas)]
    ws = [_dot_split(inv, k_beta * e_cum) for inv, k_beta, e_cum in zip(invs, k_betas, e_cums)]
    for n, (direction, hh) in enumerate(problems):
        sl = slice(hh * dk, (hh + 1) * dk)
        q = q_ref[:, sl]
        u_ref[direction, :, sl] = us[n]
        w_ref[direction, :, sl] = ws[n].astype(w_ref.dtype)
        attn_ref[direction, :, sl] = (_dot_nt(q.astype(BF16), kbs[hh]) * decays[n]).astype(attn_ref.dtype)
        qg_ref[direction, :, sl] = (q * e_cums[n]).astype(qg_ref.dtype)
        last = _lane_pick(cum_rows[n], c - 1 if direction == 0 else 0)
        kd_ref[direction, :, sl] = (ks[hh] * jnp.exp(last - cums[n])).astype(kd_ref.dtype)


def _gdn_scan_kernel(uf_ref, wf_ref, af_ref, qgf_ref, kdf_ref, cumf_ref,
                     ub_ref, wb_ref, ab_ref, qgb_ref, kdb_ref, cumb_ref, of_ref, ob_ref, state_ref,
                     *, heads):
    @pl.when(pl.program_id(1) == 0)
    def _():
        state_ref[...] = jnp.zeros_like(state_ref)

    c = uf_ref.shape[0]
    dk = GDN_HEAD_DIM
    refs = ((uf_ref, wf_ref, af_ref, qgf_ref, kdf_ref, cumf_ref, of_ref),
            (ub_ref, wb_ref, ab_ref, qgb_ref, kdb_ref, cumb_ref, ob_ref))
    chains = [(direction, hh, slice(hh * dk, (hh + 1) * dk)) for direction in range(2) for hh in range(heads)]
    states = [state_ref[direction, hh] for direction, hh, _ in chains]
    sbs = [state.astype(BF16) for state in states]
    v_news = [(refs[direction][0][:, sl] - _dot(refs[direction][1][:, sl], sb)).astype(BF16)
              for (direction, _, sl), sb in zip(chains, sbs)]
    for (direction, hh, sl), state, sb, v_new in zip(chains, states, sbs, v_news):
        _, _, attn_ref, qg_ref, kd_ref, cum_ref, o_ref = refs[direction]
        o_ref[:, sl] = _dot(attn_ref[:, sl], v_new) + _dot(qg_ref[:, sl], sb)
        last = _lane_pick(cum_ref[hh], c - 1 if direction == 0 else 0)
        state_ref[direction, hh] = state * jnp.exp(last) + _dot_tn(kd_ref[:, sl], v_new)


def _gdn(qkv, gates3, n_heads, beta_row0, cum_row0):
    s = qkv.shape[0]
    dk = GDN_HEAD_DIM
    nc = s // CHUNK
    width = n_heads * dk
    heads = _pick(n_heads, (4, 2, 1))
    wb = heads * dk
    gate4 = gates3.reshape(GATE_ROWS // n_heads, n_heads // heads, heads, 1, s)

    def gate_rows(r0):
        return pl.BlockSpec((2, None, heads, 1, CHUNK), lambda h, c: (r0 // (2 * n_heads), h, 0, 0, c))

    cols = [pl.BlockSpec((CHUNK, wb), lambda h, c, o=o: (c, o * (n_heads // heads) + h)) for o in range(3)]
    out_blk = pl.BlockSpec((2, CHUNK, wb), lambda h, c: (0, c, h))
    u, w, attn, qg, kd = pl.pallas_call(
        functools.partial(_gdn_prep_kernel, heads=heads),
        grid=(n_heads // heads, nc),
        in_specs=cols + [gate_rows(beta_row0), gate_rows(cum_row0)],
        out_specs=[out_blk] * 5,
        out_shape=[jax.ShapeDtypeStruct((2, s, width), dt) for dt in (F32, BF16, BF16, BF16, BF16)],
        compiler_params=_params("parallel", "parallel"),
        name="gdn_prep",
    )(qkv, qkv, qkv, gate4, gate4)

    heads = _pick(n_heads, (4, 2, 1))
    wb = heads * dk
    gate4 = gates3.reshape(GATE_ROWS // n_heads, n_heads // heads, heads, 1, s)

    def specs(direction, chunk):
        seq = pl.BlockSpec((None, CHUNK, wb), lambda h, c: (direction, chunk(c), h))
        cum = pl.BlockSpec((None, None, heads, 1, CHUNK),
                           lambda h, c: (cum_row0 // n_heads + direction, h, 0, 0, chunk(c)))
        return [seq] * 5 + [cum]

    out_f, out_b = _both_directions(lambda chunk: pl.BlockSpec((CHUNK, wb), lambda h, c: (chunk(c), h)), nc)
    ops = (u, w, attn, qg, kd, gate4)
    return pl.pallas_call(
        functools.partial(_gdn_scan_kernel, heads=heads),
        grid=(n_heads // heads, nc),
        in_specs=specs(0, lambda c: c) + specs(1, lambda c: nc - 1 - c),
        out_specs=[out_f, out_b],
        out_shape=[jax.ShapeDtypeStruct((s, width), F32)] * 2,
        scratch_shapes=[pltpu.VMEM((2, heads, dk, dk), F32)],
        compiler_params=_params("parallel", "arbitrary"),
        name="gdn_scan",
    )(*ops, *ops)


def _mlstm_kernel(qf_ref, kf_ref, vf_ref, igf_ref, cumf_ref, qb_ref, kb_ref, vb_ref, igb_ref, cumb_ref,
                  of_ref, ob_ref, mem_ref, nrm_ref, m_ref, *, heads):
    @pl.when(pl.program_id(1) == 0)
    def _():
        mem_ref[...] = jnp.zeros_like(mem_ref)
        nrm_ref[...] = jnp.zeros_like(nrm_ref)
        m_ref[...] = jnp.zeros_like(m_ref)

    c = qf_ref.shape[0]
    dk = MLSTM_HEAD_DIM
    refs = ((qf_ref, kf_ref, vf_ref, igf_ref, cumf_ref, of_ref),
            (qb_ref, kb_ref, vb_ref, igb_ref, cumb_ref, ob_ref))
    chains = []
    for direction, (q_ref, k_ref, v_ref, ig_ref, cum_ref, o_ref) in enumerate(refs):
        incl, _ = _dir_masks(c, direction)
        for hh in range(heads):
            sl = slice(hh * dk, (hh + 1) * dk)
            q = q_ref[:, sl]
            k = k_ref[:, sl] * (dk ** -0.5)
            vb = v_ref[:, sl].astype(BF16)
            qb = q.astype(BF16)
            ig_row = ig_ref[hh]
            cum_row = cum_ref[hh]
            ig = _row_to_col(ig_row)
            cum = _row_to_col(cum_row)
            log_d = jnp.where(incl, cum - cum_row + ig_row, -jnp.inf)
            d_max = jnp.max(log_d, axis=1, keepdims=True)
            qk = _dot_nt(qb, k.astype(BF16))
            b_last = _lane_pick(cum_row, c - 1 if direction == 0 else 0)
            log_w = b_last - cum + ig
            w_max = jnp.max(log_w, axis=0, keepdims=True)
            chains.append((direction, hh, sl, o_ref, q, k, vb, qb, cum, log_d, d_max, qk, b_last, log_w, w_max))
    carried = [(m_ref[direction, hh, 0:1, 0:1], mem_ref[direction, hh], nrm_ref[direction, hh])
               for direction, hh, *_ in chains]
    for (direction, hh, sl, o_ref, q, k, vb, qb, cum, log_d, d_max, qk, *_), (m_prev, mem, nrm) in zip(
            chains, carried):
        m_inter = cum + m_prev
        m_row = jnp.maximum(m_inter, d_max)
        s_mat = qk * jnp.exp(log_d - m_row)
        inter_scale = jnp.exp(m_inter - m_row)
        num = _dot(s_mat.astype(BF16), vb) + inter_scale * _dot(qb, mem.astype(BF16))
        den = (jnp.sum(s_mat, axis=1, keepdims=True)
               + inter_scale * jnp.sum(q * nrm, axis=1, keepdims=True))
        o_ref[:, sl] = num / jnp.maximum(jnp.abs(den), jnp.exp(-m_row))
    for (direction, hh, sl, o_ref, q, k, vb, qb, cum, log_d, d_max, qk, b_last, log_w, w_max), (
            m_prev, mem, nrm) in zip(chains, carried):
        m_new = jnp.maximum(b_last + m_prev, w_max)
        kw = k * jnp.exp(log_w - m_new)
        carry_scale = jnp.exp(b_last + m_prev - m_new)
        mem_ref[direction, hh] = carry_scale * mem + _dot_tn(kw.astype(BF16), vb)
        nrm_ref[direction, hh] = carry_scale * nrm + jnp.sum(kw, axis=0, keepdims=True)
        m_ref[direction, hh] = jnp.broadcast_to(m_new, m_ref.shape[2:])


def _mlstm(p, offs, gates3, n_heads, ig_row0, cum_row0):
    s = p.shape[0]
    dk = MLSTM_HEAD_DIM
    nc = s // CHUNK
    heads = _pick(n_heads, (2, 1))
    wb = heads * dk
    qo, ko, vo = (o // wb for o in offs)
    gate4 = gates3.reshape(GATE_ROWS // n_heads, n_heads // heads, heads, 1, s)

    def specs(direction, chunk):
        col = [pl.BlockSpec((CHUNK, wb), lambda h, c, o=o: (chunk(c), o + h)) for o in (qo, ko, vo)]
        rows = [pl.BlockSpec((None, None, heads, 1, CHUNK),
                             lambda h, c, r0=r0: (r0 // n_heads + direction, h, 0, 0, chunk(c)))
                for r0 in (ig_row0, cum_row0)]
        return col + rows

    out_f, out_b = _both_directions(lambda chunk: pl.BlockSpec((CHUNK, wb), lambda h, c: (chunk(c), h)), nc)
    return pl.pallas_call(
        functools.partial(_mlstm_kernel, heads=heads),
        grid=(n_heads // heads, nc),
        in_specs=specs(0, lambda c: c) + specs(1, lambda c: nc - 1 - c),
        out_specs=[out_f, out_b],
        out_shape=[jax.ShapeDtypeStruct((s, n_heads * dk), F32)] * 2,
        scratch_shapes=[pltpu.VMEM((2, heads, dk, dk), F32), pltpu.VMEM((2, heads, 1, dk), F32),
                        pltpu.VMEM((2, heads, 8, 128), F32)],
        compiler_params=_params("parallel", "arbitrary"),
        name="mlstm_scan",
    )(p, p, p, gate4, gate4, p, p, p, gate4, gate4)


def _finalize_kernel(yf_ref, yb_ref, g_ref, w_ref, o_ref, *, head_dim, subtract_mean, swish_gate):
    y = yf_ref[...] + yb_ref[...]
    g = g_ref[...]
    gate = jax.nn.sigmoid(g)
    if swish_gate:
        gate = g * gate
    w = w_ref[...]
    for hh in range(y.shape[1] // head_dim):
        sl = slice(hh * head_dim, (hh + 1) * head_dim)
        t = y[:, sl]
        if subtract_mean:
            t = t - jnp.mean(t, axis=-1, keepdims=True)
        t = t * lax.rsqrt(jnp.mean(t * t, axis=-1, keepdims=True) + EPS)
        o_ref[:, sl] = (t * w[:, sl] * gate[:, sl]).astype(o_ref.dtype)


def _finalize(y2, p, gate_off, norm_w, head_dim, subtract_mean, swish_gate):
    y_f, y_b = y2
    s, width = y_f.shape
    tm = _pick(s, (256, 128))
    tc = _pick(width, (512, 256))
    go = gate_off // tc
    blk = pl.BlockSpec((tm, tc), lambda i, j: (i, j))
    return pl.pallas_call(
        functools.partial(_finalize_kernel, head_dim=head_dim, subtract_mean=subtract_mean,
                          swish_gate=swish_gate),
        grid=(s // tm, width // tc),
        in_specs=[blk, blk, pl.BlockSpec((tm, tc), lambda i, j: (i, go + j)),
                  pl.BlockSpec((1, tc), lambda i, j: (0, j))],
        out_specs=blk,
        out_shape=jax.ShapeDtypeStruct((s, width), BF16),
        compiler_params=_params("parallel", "parallel"),
        name="branch_finalize",
    )(y_f, y_b, p, norm_w.reshape(1, width))


def _merge_kernel(ya_ref, yb_ref, yc_ref, wb_ref, la_ref, lb_ref, lc_ref, o_ref):
    acc = jax.nn.sigmoid(la_ref[...]) * _dot(ya_ref[...], wb_ref[0])
    acc = acc + jax.nn.sigmoid(lb_ref[...]) * _dot(yb_ref[...], wb_ref[1])
    acc = acc + jax.nn.sigmoid(lc_ref[...]) * _dot(yc_ref[...], wb_ref[2])
    o_ref[...] = acc.astype(o_ref.dtype)


def _merge(ys, w_branch, logits, d_model):
    s, width = ys[0].shape
    tm = _pick(s, (512, 256, 128))
    tn = _pick(d_model, (512, 256, 128))
    nb = d_model // tn
    y_blk = pl.BlockSpec((tm, width), lambda i, j: (i, 0))

    def logit(r):
        return pl.BlockSpec((tm, tn), lambda i, j: (i, r * nb + j))

    return pl.pallas_call(
        _merge_kernel,
        grid=(s // tm, nb),
        in_specs=[y_blk, y_blk, y_blk, pl.BlockSpec((N_BRANCHES, width, tn), lambda i, j: (0, 0, j)),
                  logit(0), logit(1), logit(2)],
        out_specs=pl.BlockSpec((tm, tn), lambda i, j: (i, j)),
        out_shape=jax.ShapeDtypeStruct((s, d_model), BF16),
        compiler_params=_params("parallel", "parallel"),
        name="merge",
    )(*ys, w_branch, logits, logits, logits)


def kernel(x, positions, norm_mix, w_in, conv_w, gdn_a_log, gdn_dt_bias, mlstm_i_bias, mlstm_f_bias,
           ret_norm, gdn_norm, mlstm_norm, w_branch, w_out, norm_ffn, w_ffn_in, w_ffn_out, norm_final):
    batch, s, d = x.shape
    assert batch == 1 and s % CHUNK == 0
    depth = w_in.shape[0]
    mix = d // 2
    ret_heads = mix // RET_HEAD_DIM
    gdn_heads = mix // GDN_HEAD_DIM
    mlstm_heads = mix // MLSTM_HEAD_DIM
    d_ff = w_ffn_out.shape[1]
    assert 4 * gdn_heads + 4 * mlstm_heads <= GATE_ROWS and GATE_ROWS % (2 * gdn_heads) == 0

    sizes = (mix, mix, mix, mix, 3 * mix, mix, 2 * gdn_heads, 2 * gdn_heads,
             mix, mix, mix, mix, 2 * mlstm_heads, 2 * mlstm_heads, N_BRANCHES * d)
    starts = np.concatenate([[0], np.cumsum(sizes)]).tolist()
    n_lead = starts[6]
    small = (6, 7, 12, 13)
    n_small = sum(sizes[idx] for idx in small)

    x2 = x.reshape(s, d)
    tables = _rope_table(positions.reshape(s), RET_HEAD_DIM // 2)
    w_in_t = jnp.swapaxes(w_in, 1, 2)

    for l in range(depth):
        w_small_t = jnp.concatenate([w_in_t[l, starts[idx]:starts[idx + 1], :] for idx in small], axis=0)
        w_small_t = jnp.pad(w_small_t, ((0, GATE_ROWS - n_small), (0, 0))).astype(BF16)
        zeros = jnp.zeros((2 * gdn_heads,), F32)
        gate_bias = jnp.concatenate([zeros, gdn_dt_bias[l].reshape(-1), mlstm_i_bias[l].reshape(-1),
                                     mlstm_f_bias[l].reshape(-1)])
        gate_scale = jnp.concatenate([zeros + 1.0, -jnp.exp(gdn_a_log[l].reshape(-1)),
                                      jnp.ones((4 * mlstm_heads,), F32)])
        gate_par = jnp.pad(jnp.stack([gate_bias, gate_scale], axis=1), ((0, GATE_ROWS - n_small), (0, 0)))

        h = _rmsnorm(x2, norm_mix[l], BF16)
        p_lead = _matmul_wt(h, w_in_t, l, 0, n_lead, "in_proj_lead")
        p_mlstm = _matmul_wt(h, w_in_t, l, starts[8], 4 * mix, "in_proj_mlstm")
        logits = _matmul_wt(h, w_in_t, l, starts[14], N_BRANCHES * d, "in_proj_logits")
        gates3 = _gates(h, w_small_t, gate_par, gdn_heads, mlstm_heads).reshape(GATE_ROWS, 1, s)

        y_ret = _retention(p_lead, (starts[0], starts[1], starts[2]), tables, ret_heads)
        y_ret = _finalize(y_ret, p_lead, starts[3], ret_norm[l], RET_HEAD_DIM, True, True)

        qkv = _gdn_conv(p_lead, starts[4], 3 * mix, conv_w[l])
        y_gdn = _gdn(qkv, gates3, gdn_heads, 0, 2 * gdn_heads)
        y_gdn = _finalize(y_gdn, p_lead, starts[5], gdn_norm[l], GDN_HEAD_DIM, False, True)

        y_ml = _mlstm(p_mlstm, (0, mix, 2 * mix), gates3, mlstm_heads,
                      4 * gdn_heads, 4 * gdn_heads + 2 * mlstm_heads)
        y_ml = _finalize(y_ml, p_mlstm, 3 * mix, mlstm_norm[l], MLSTM_HEAD_DIM, False, False)

        merged = _merge((y_ret, y_gdn, y_ml), w_branch[l].astype(BF16), logits, d)
        x2 = _matmul_f32w(merged, w_out, l, d, "out_proj", res=x2)

        h = _rmsnorm(x2, norm_ffn[l], BF16)
        act = _ffn_in(h, w_ffn_in, l, d_ff)
        x2 = _matmul_residual(act, w_ffn_out[l].astype(BF16), x2, "ffn_out")

    return _rmsnorm(x2, norm_final, x.dtype).reshape(batch, s, d)
```

```python
import functools

import numpy as np
import jax
import jax.numpy as jnp
from jax import lax
from jax.experimental import pallas as pl
from jax.experimental.pallas import tpu as pltpu

RET_HEAD_DIM = 256
GDN_HEAD_DIM = 128
MLSTM_HEAD_DIM = 256
CONV_WIDTH = 5
ROPE_BASE = 10000.0
N_BRANCHES = 3
EPS = 1e-6

CHUNK = 128
GATE_ROWS = 128
F32_SUBLANES = 8
V7X_VMEM_BYTES = 64 * 1024 * 1024
VMEM_LIMIT = V7X_VMEM_BYTES - 8 * 1024 * 1024
VMEM_LIMIT_WIDE_PANEL = V7X_VMEM_BYTES - 4 * 1024 * 1024

F32 = jnp.float32
BF16 = jnp.bfloat16


def _params(*sem, vmem=VMEM_LIMIT):
    return pltpu.CompilerParams(dimension_semantics=sem, vmem_limit_bytes=vmem)


def _pick(n, candidates):
    for c in candidates:
        if n % c == 0:
            return c
    raise ValueError(f"no tile in {candidates} divides {n}")


def _dot(a, b):
    return jnp.dot(a, b, preferred_element_type=F32)


def _dot_nt(a, b):
    return lax.dot_general(a, b, (((1,), (1,)), ((), ())), preferred_element_type=F32)


def _dot_tn(a, b):
    return lax.dot_general(a, b, (((0,), (0,)), ((), ())), preferred_element_type=F32)


def _dot_f32(a, b):
    return jnp.dot(a, b, preferred_element_type=F32, precision=lax.Precision.HIGHEST)


def _split(a):
    hi = a.astype(BF16)
    return hi, (a - hi.astype(F32)).astype(BF16)


def _dot_split(a, b):
    a_hi, a_lo = _split(a)
    b_hi, b_lo = _split(b)
    return _dot(a_hi, b_hi) + (_dot(a_hi, b_lo) + _dot(a_lo, b_hi))


def _row_to_col(row):
    c = row.shape[-1]
    i = lax.broadcasted_iota(jnp.int32, (c, c), 0)
    j = lax.broadcasted_iota(jnp.int32, (c, c), 1)
    return jnp.sum(jnp.where(i == j, jnp.broadcast_to(row, (c, c)), 0.0), axis=1, keepdims=True)


def _lane_pick(row, idx):
    lane = lax.broadcasted_iota(jnp.int32, row.shape, 1)
    return jnp.sum(jnp.where(lane == idx, row, 0.0), axis=1, keepdims=True)


def _dir_masks(c, direction):
    i = lax.broadcasted_iota(jnp.int32, (c, c), 0)
    j = lax.broadcasted_iota(jnp.int32, (c, c), 1)
    if direction == 0:
        return j <= i, j < i
    return j >= i, j > i


def _rmsnorm_kernel(x_ref, w_ref, o_ref):
    x = x_ref[...]
    y = x * lax.rsqrt(jnp.mean(x * x, axis=-1, keepdims=True) + EPS)
    o_ref[...] = (y * w_ref[...]).astype(o_ref.dtype)


def _rmsnorm(x, w, out_dtype):
    s, d = x.shape
    tm = _pick(s, (256, 128))
    return pl.pallas_call(
        _rmsnorm_kernel,
        grid=(s // tm,),
        in_specs=[pl.BlockSpec((tm, d), lambda i: (i, 0)), pl.BlockSpec((1, d), lambda i: (0, 0))],
        out_specs=pl.BlockSpec((tm, d), lambda i: (i, 0)),
        out_shape=jax.ShapeDtypeStruct((s, d), out_dtype),
        compiler_params=_params("parallel"),
        name="rmsnorm",
    )(x, w.reshape(1, d))


def _mm_kernel(a_ref, b_ref, o_ref):
    o_ref[...] = _dot(a_ref[...], b_ref[...]).astype(o_ref.dtype)


def _mm_cast_kernel(a_ref, w_ref, o_ref, wb_ref):
    @pl.when(pl.program_id(1) == 0)
    def _():
        wb_ref[...] = w_ref[...].astype(BF16)

    o_ref[...] = _dot(a_ref[...], wb_ref[...]).astype(o_ref.dtype)


def _mm_cast_res_kernel(a_ref, w_ref, r_ref, o_ref, wb_ref):
    @pl.when(pl.program_id(1) == 0)
    def _():
        wb_ref[...] = w_ref[...].astype(BF16)

    o_ref[...] = r_ref[...] + _dot(a_ref[...], wb_ref[...])


def _matmul_bf16w(a, b, out_dtype, name):
    m, k = a.shape
    n = b.shape[1]
    tm = _pick(m, (1024, 512, 256, 128))
    tn = _pick(n, (1024, 512, 256, 128))
    return pl.pallas_call(
        _mm_kernel,
        grid=(n // tn, m // tm),
        in_specs=[pl.BlockSpec((tm, k), lambda j, i: (i, 0)), pl.BlockSpec((k, tn), lambda j, i: (0, j))],
        out_specs=pl.BlockSpec((tm, tn), lambda j, i: (i, j)),
        out_shape=jax.ShapeDtypeStruct((m, n), out_dtype),
        compiler_params=_params("parallel", "parallel"),
        name=name,
    )(a, b)


def _matmul_f32w(a, w_stack, layer, n, name, res=None):
    m, k = a.shape
    tm = _pick(m, (1024, 512, 256, 128))
    tn = _pick(n, (512, 256, 128))
    in_specs = [pl.BlockSpec((tm, k), lambda j, i: (i, 0)),
                pl.BlockSpec((None, k, tn), lambda j, i: (layer, 0, j))]
    operands = [a, w_stack]
    if res is not None:
        in_specs.append(pl.BlockSpec((tm, tn), lambda j, i: (i, j)))
        operands.append(res)
    return pl.pallas_call(
        _mm_cast_kernel if res is None else _mm_cast_res_kernel,
        grid=(n // tn, m // tm),
        in_specs=in_specs,
        out_specs=pl.BlockSpec((tm, tn), lambda j, i: (i, j)),
        out_shape=jax.ShapeDtypeStruct((m, n), F32),
        scratch_shapes=[pltpu.VMEM((k, tn), BF16)],
        compiler_params=_params("parallel", "arbitrary"),
        name=name,
    )(*operands)


def _mm_wt_kernel(a_ref, wt_ref, o_ref, wb_ref):
    @pl.when(pl.program_id(1) == 0)
    def _():
        wb_ref[...] = wt_ref[0].astype(BF16)

    o_ref[...] = _dot_nt(a_ref[...], wb_ref[...]).astype(o_ref.dtype)


def _matmul_wt(a, wt_stack, layer, row0, n, name):
    m, k = a.shape
    tm = _pick(m, (512, 256, 128))
    tn = _pick(n, (1024, 512, 256, 128))
    return pl.pallas_call(
        _mm_wt_kernel,
        grid=(n // tn, m // tm),
        in_specs=[pl.BlockSpec((tm, k), lambda j, i: (i, 0)),
                  pl.BlockSpec((pl.Element(1), pl.Element(tn), pl.Element(k)),
                               lambda j, i: (layer, pl.multiple_of(row0 + j * tn, F32_SUBLANES), 0))],
        out_specs=pl.BlockSpec((tm, tn), lambda j, i: (i, j)),
        out_shape=jax.ShapeDtypeStruct((m, n), F32),
        scratch_shapes=[pltpu.VMEM((tn, k), BF16)],
        compiler_params=_params("parallel", "arbitrary", vmem=VMEM_LIMIT_WIDE_PANEL),
        name=name,
    )(a, wt_stack)


def _mm_res_kernel(a_ref, b_ref, r_ref, o_ref):
    o_ref[...] = r_ref[...] + _dot(a_ref[...], b_ref[...])


def _matmul_residual(a, b_stack, layer, res, name):
    m, k = a.shape
    n = b_stack.shape[2]
    tm = _pick(m, (1024, 512, 256, 128))
    tn = _pick(n, (256, 128))
    return pl.pallas_call(
        _mm_res_kernel,
        grid=(m // tm, n // tn),
        in_specs=[pl.BlockSpec((tm, k), lambda i, j: (i, 0), pipeline_mode=pl.Buffered(1)),
                  pl.BlockSpec((None, k, tn), lambda i, j: (layer, 0, j)),
                  pl.BlockSpec((tm, tn), lambda i, j: (i, j))],
        out_specs=pl.BlockSpec((tm, tn), lambda i, j: (i, j)),
        out_shape=jax.ShapeDtypeStruct((m, n), F32),
        compiler_params=_params("parallel", "arbitrary"),
        name=name,
    )(a, b_stack, res)


def _swiglu_kernel(a_ref, wg_ref, wu_ref, o_ref, wgb_ref, wub_ref):
    @pl.when(pl.program_id(1) == 0)
    def _():
        wgb_ref[...] = wg_ref[...].astype(BF16)
        wub_ref[...] = wu_ref[...].astype(BF16)

    a = a_ref[...]
    gate = _dot(a, wgb_ref[...])
    up = _dot(a, wub_ref[...])
    o_ref[...] = (gate * jax.nn.sigmoid(gate) * up).astype(o_ref.dtype)


def _ffn_in(h, w_stack, layer, d_ff):
    m, k = h.shape
    tm = _pick(m, (1024, 512, 256, 128))
    tn = _pick(d_ff, (256, 128))
    nb = d_ff // tn
    return pl.pallas_call(
        _swiglu_kernel,
        grid=(nb, m // tm),
        in_specs=[pl.BlockSpec((tm, k), lambda j, i: (i, 0)),
                  pl.BlockSpec((None, k, tn), lambda j, i: (layer, 0, j)),
                  pl.BlockSpec((None, k, tn), lambda j, i: (layer, 0, j + nb))],
        out_specs=pl.BlockSpec((tm, tn), lambda j, i: (i, j)),
        out_shape=jax.ShapeDtypeStruct((m, d_ff), BF16),
        scratch_shapes=[pltpu.VMEM((k, tn), BF16), pltpu.VMEM((k, tn), BF16)],
        compiler_params=_params("parallel", "arbitrary"),
        name="ffn_in",
    )(h, w_stack, w_stack)


def _gates_kernel(w_ref, h_ref, par_ref, o_ref, *, n_gdn, n_mlstm):
    z = _dot_nt(w_ref[...].astype(BF16), h_ref[...])
    r, tm = z.shape
    bias = par_ref[:, 0:1]
    scale = par_ref[:, 1:2]
    row = lax.broadcasted_iota(jnp.int32, (r, tm), 0)
    zb = z + bias
    soft = jnp.log1p(jnp.exp(-jnp.abs(zb)))
    softplus = jnp.maximum(zb, 0.0) + soft
    log_sig = jnp.minimum(zb, 0.0) - soft
    b0, b1, b2, b3 = 2 * n_gdn, 4 * n_gdn, 4 * n_gdn + 2 * n_mlstm, 4 * n_gdn + 4 * n_mlstm
    is_alpha = (row >= b0) & (row < b1)
    is_logf = (row >= b2) & (row < b3)
    raw = jnp.where(row < b0, jax.nn.sigmoid(zb),
                    jnp.where(is_alpha, scale * softplus,
                              jnp.where(row < b2, zb, jnp.where(is_logf, log_sig, 0.0))))
    pj = lax.broadcasted_iota(jnp.int32, (tm, tm), 0)
    pi = lax.broadcasted_iota(jnp.int32, (tm, tm), 1)
    same = (pj // CHUNK) == (pi // CHUNK)
    prefix = jnp.where(same & (pj <= pi), 1.0, 0.0)
    suffix = jnp.where(same & (pj >= pi), 1.0, 0.0)
    cum_p = _dot_f32(raw, prefix)
    cum_s = _dot_f32(raw, suffix)
    dir1 = (is_alpha & (row >= b0 + n_gdn)) | (is_logf & (row >= b2 + n_mlstm))
    cum = jnp.where(dir1, cum_s, cum_p)
    o_ref[...] = jnp.where(is_alpha | is_logf, cum, raw)


def _gates(h, w, par, n_gdn, n_mlstm):
    s, d = h.shape
    tm = _pick(s, (256, 128))
    return pl.pallas_call(
        functools.partial(_gates_kernel, n_gdn=n_gdn, n_mlstm=n_mlstm),
        grid=(s // tm,),
        in_specs=[pl.BlockSpec((GATE_ROWS, d), lambda i: (0, 0)),
                  pl.BlockSpec((tm, d), lambda i: (i, 0)),
                  pl.BlockSpec((GATE_ROWS, 2), lambda i: (0, 0))],
        out_specs=pl.BlockSpec((GATE_ROWS, tm), lambda i: (0, i)),
        out_shape=jax.ShapeDtypeStruct((GATE_ROWS, s), F32),
        compiler_params=_params("parallel"),
        name="gates",
    )(w, h, par)


def _rope_kernel(pos_ref, freq_ref, cos_ref, sin_ref):
    ang = pos_ref[...].astype(F32) * freq_ref[...]
    cos_ref[...] = jnp.cos(ang)
    sin_ref[...] = jnp.sin(ang)


def _rope_table(positions, half):
    s = positions.shape[0]
    tm = _pick(s, (512, 256, 128))
    inv_freq = (ROPE_BASE ** (-jnp.arange(half, dtype=F32) / half)).reshape(1, half)
    return pl.pallas_call(
        _rope_kernel,
        grid=(s // tm,),
        in_specs=[pl.BlockSpec((tm, 1), lambda i: (i, 0)), pl.BlockSpec((1, half), lambda i: (0, 0))],
        out_specs=[pl.BlockSpec((tm, half), lambda i: (i, 0))] * 2,
        out_shape=[jax.ShapeDtypeStruct((s, half), F32)] * 2,
        compiler_params=_params("parallel"),
        name="rope_table",
    )(positions.reshape(s, 1), inv_freq)


def _both_directions(make_spec, nc):
    return [make_spec(lambda c: c), make_spec(lambda c: nc - 1 - c)]


def _retention_kernel(lg_ref, qf_ref, kf_ref, vf_ref, cosf_ref, sinf_ref,
                      qb_ref, kb_ref, vb_ref, cosb_ref, sinb_ref, of_ref, ob_ref, state_ref, *, heads):
    hg = pl.program_id(0)

    @pl.when(pl.program_id(1) == 0)
    def _():
        state_ref[...] = jnp.zeros_like(state_ref)

    c = qf_ref.shape[0]
    dk = RET_HEAD_DIM
    half = dk // 2
    i = lax.broadcasted_iota(jnp.int32, (c, c), 0)
    j = lax.broadcasted_iota(jnp.int32, (c, c), 1)
    dist = jnp.abs(i - j).astype(F32)
    idx = lax.broadcasted_iota(jnp.int32, (c, 1), 0).astype(F32)
    refs = ((qf_ref, kf_ref, vf_ref, cosf_ref, sinf_ref, of_ref),
            (qb_ref, kb_ref, vb_ref, cosb_ref, sinb_ref, ob_ref))
    chains = []
    for direction, (q_ref, k_ref, v_ref, cos_ref, sin_ref, o_ref) in enumerate(refs):
        cos = cos_ref[...]
        sin = sin_ref[...]

        def rot(t, cos=cos, sin=sin):
            t1, t2 = t[:, :half], t[:, half:]
            return jnp.concatenate([t1 * cos - t2 * sin, t1 * sin + t2 * cos], axis=1)

        mask = (j <= i) if direction == 0 else (j > i)
        for hh in range(heads):
            sl = slice(hh * dk, (hh + 1) * dk)
            lg = jnp.full((1, 1), lg_ref[hg * heads + hh], F32)
            decay_intra = jnp.where(mask, jnp.exp(lg * jnp.where(mask, dist, 0.0)), 0.0)
            q_decay = jnp.exp(lg * ((idx + 1.0) if direction == 0 else (c - idx)))
            k_decay = jnp.exp(lg * ((c - 1.0 - idx) if direction == 0 else idx))
            q = rot(q_ref[:, sl])
            k = rot(k_ref[:, sl]) * (dk ** -0.5)
            v = v_ref[:, sl].astype(BF16)
            scores = (_dot_nt(q.astype(BF16), k.astype(BF16)) * decay_intra).astype(BF16)
            chains.append((direction, hh, sl, o_ref, scores, v, (q * q_decay).astype(BF16),
                           (k * k_decay).astype(BF16), jnp.exp(lg * c)))
    states = [state_ref[direction, hh] for direction, hh, *_ in chains]
    for (direction, hh, sl, o_ref, scores, v, qd, kd, _), state in zip(chains, states):
        o_ref[:, sl] = _dot(scores, v) + _dot(qd, state.astype(BF16))
    for (direction, hh, sl, o_ref, scores, v, qd, kd, chunk_decay), state in zip(chains, states):
        state_ref[direction, hh] = state * chunk_decay + _dot_tn(kd, v)


def _retention(p, offs, tables, n_heads):
    s = p.shape[0]
    dk = RET_HEAD_DIM
    nc = s // CHUNK
    heads = _pick(n_heads, (4, 2, 1))
    wb = heads * dk
    cos, sin = tables
    qo, ko, vo = (o // wb for o in offs)
    log_gamma = jnp.log1p(-jnp.exp2(-5.0 - jnp.arange(n_heads, dtype=F32)))

    def specs(chunk):
        col = [pl.BlockSpec((CHUNK, wb), lambda h, c, o=o: (chunk(c), o + h)) for o in (qo, ko, vo)]
        tab = pl.BlockSpec((CHUNK, dk // 2), lambda h, c: (chunk(c), 0))
        return col + [tab, tab]

    fwd, bwd = _both_directions(specs, nc)
    out_f, out_b = _both_directions(lambda chunk: pl.BlockSpec((CHUNK, wb), lambda h, c: (chunk(c), h)), nc)
    return pl.pallas_call(
        functools.partial(_retention_kernel, heads=heads),
        grid=(n_heads // heads, nc),
        in_specs=[pl.BlockSpec(memory_space=pltpu.SMEM)] + fwd + bwd,
        out_specs=[out_f, out_b],
        out_shape=[jax.ShapeDtypeStruct((s, n_heads * dk), F32)] * 2,
        scratch_shapes=[pltpu.VMEM((2, heads, dk, dk), F32)],
        compiler_params=_params("parallel", "arbitrary"),
        name="retention_scan",
    )(log_gamma, p, p, p, cos, sin, p, p, p, cos, sin)


def _conv_kernel(prev_ref, cur_ref, next_ref, w_ref, o_ref, pad_ref, *, n_qk_blocks, n_q_blocks, halo):
    i = pl.program_id(0)
    jb = pl.program_id(1)
    tr, tc = cur_ref.shape
    keep_prev = jnp.where(i == 0, 0.0, 1.0)
    keep_next = jnp.where(i == pl.num_programs(0) - 1, 0.0, 1.0)
    pad_ref[0:halo, :] = prev_ref[...] * keep_prev
    pad_ref[halo:halo + tr, :] = cur_ref[...]
    pad_ref[halo + tr:, :] = next_ref[...] * keep_next
    side = (CONV_WIDTH - 1) // 2
    acc = jnp.zeros((tr, tc), F32)
    for tap in range(CONV_WIDTH):
        acc = acc + pad_ref[pl.ds(halo - side + tap, tr), :] * w_ref[tap:tap + 1, :]
    y = acc * jax.nn.sigmoid(acc)
    is_qk = jb < n_qk_blocks
    q_scale = jnp.where(jb < n_q_blocks, GDN_HEAD_DIM ** -0.5, 1.0)
    for g in range(tc // GDN_HEAD_DIM):
        sl = slice(g * GDN_HEAD_DIM, (g + 1) * GDN_HEAD_DIM)
        t = y[:, sl]
        normed = t * lax.rsqrt(jnp.sum(t * t, axis=-1, keepdims=True) + EPS) * q_scale
        o_ref[:, sl] = jnp.where(is_qk, normed, t)


def _gdn_conv(p, off, width, conv_w):
    s = p.shape[0]
    halo = 8
    tr = _pick(s, (1024, 512, 256, 128))
    tc = _pick(width // 3, (512, 256, 128))
    ob = off // tc
    n_row8 = s // halo
    return pl.pallas_call(
        functools.partial(_conv_kernel, n_qk_blocks=2 * (width // 3) // tc, n_q_blocks=(width // 3) // tc,
                          halo=halo),
        grid=(s // tr, width // tc),
        in_specs=[pl.BlockSpec((halo, tc), lambda i, j: (jnp.maximum(i * (tr // halo) - 1, 0), ob + j)),
                  pl.BlockSpec((tr, tc), lambda i, j: (i, ob + j)),
                  pl.BlockSpec((halo, tc), lambda i, j: (jnp.minimum((i + 1) * (tr // halo), n_row8 - 1), ob + j)),
                  pl.BlockSpec((CONV_WIDTH, tc), lambda i, j: (0, j))],
        out_specs=pl.BlockSpec((tr, tc), lambda i, j: (i, j)),
        out_shape=jax.ShapeDtypeStruct((s, width), F32),
        scratch_shapes=[pltpu.VMEM((tr + 2 * halo, tc), F32)],
        compiler_params=_params("parallel", "parallel"),
        name="gdn_conv",
    )(p, p, p, conv_w)


def _unit_triangular_inverses(lms):
    c = lms[0].shape[0]
    i = lax.broadcasted_iota(jnp.int32, (c, c), 0)
    j = lax.broadcasted_iota(jnp.int32, (c, c), 1)
    level = jnp.bitwise_xor(i, j)
    eye = jnp.where(i == j, 1.0, 0.0)
    xs = [eye - jnp.where(level == 1, lm, 0.0) for lm in lms]
    span = 2
    while span < c:
        joins = (level >= span) & (level < 2 * span)
        ys = [_dot(jnp.where(joins, lm, 0.0).astype(BF16), x.astype(BF16)) for lm, x in zip(lms, xs)]
        xs = [x - _dot(x.astype(BF16), y.astype(BF16)) for x, y in zip(xs, ys)]
        span *= 2
    resids = [eye - x - _dot_split(lm, x) for lm, x in zip(lms, xs)]
    return [x + _dot(x.astype(BF16), r.astype(BF16)) for x, r in zip(xs, resids)]


def _gdn_prep_kernel(q_ref, k_ref, v_ref, beta_ref, cum_ref, u_ref, w_ref, attn_ref, qg_ref, kd_ref,
                     *, heads):
    c = q_ref.shape[0]
    dk = GDN_HEAD_DIM
    problems = [(direction, hh) for direction in range(2) for hh in range(heads)]
    masks = [_dir_masks(c, direction) for direction in range(2)]
    ks = [k_ref[:, hh * dk:(hh + 1) * dk] for hh in range(heads)]
    kbs = [k.astype(BF16) for k in ks]
    cum_rows = [cum_ref[direction, hh] for direction, hh in problems]
    cums = [_row_to_col(row) for row in cum_rows]
    betas = [_row_to_col(beta_ref[direction, hh]) for direction, hh in problems]
    decays = []
    for (direction, _), cum, row in zip(problems, cums, cum_rows):
        incl = masks[direction][0]
        decays.append(jnp.where(incl, jnp.exp(jnp.where(incl, cum - row, 0.0)), 0.0))
    k_betas = [ks[hh] * beta for (_, hh), beta in zip(problems, betas)]
    lowers = [_dot_nt(k_beta.astype(BF16), kbs[hh]) * jnp.where(masks[direction][1], decay, 0.0)
              for (direction, hh), k_beta, decay in zip(problems, k_betas, decays)]
    invs = _unit_triangular_inverses(lowers)
    e_cums = [jnp.exp(cum) for cum in cums]
    us = [_dot_split(inv, v_ref[:, hh * dk:(hh + 1) * dk] * beta)
          for (_, hh), inv, beta in zip(problems, invs, betas)]
    ws = [_dot_split(inv, k_beta * e_cum) for inv, k_beta, e_cum in zip(invs, k_betas, e_cums)]
    for n, (direction, hh) in enumerate(problems):
        sl = slice(hh * dk, (hh + 1) * dk)
        q = q_ref[:, sl]
        u_ref[direction, :, sl] = us[n]
        w_ref[direction, :, sl] = ws[n].astype(w_ref.dtype)
        attn_ref[direction, :, sl] = (_dot_nt(q.astype(BF16), kbs[hh]) * decays[n]).astype(attn_ref.dtype)
        qg_ref[direction, :, sl] = (q * e_cums[n]).astype(qg_ref.dtype)
        last = _lane_pick(cum_rows[n], c - 1 if direction == 0 else 0)
        kd_ref[direction, :, sl] = (ks[hh] * jnp.exp(last - cums[n])).astype(kd_ref.dtype)


def _gdn_scan_kernel(uf_ref, wf_ref, af_ref, qgf_ref, kdf_ref, cumf_ref,
                     ub_ref, wb_ref, ab_ref, qgb_ref, kdb_ref, cumb_ref, of_ref, ob_ref, state_ref,
                     *, heads):
    @pl.when(pl.program_id(1) == 0)
    def _():
        state_ref[...] = jnp.zeros_like(state_ref)

    c = uf_ref.shape[0]
    dk = GDN_HEAD_DIM
    refs = ((uf_ref, wf_ref, af_ref, qgf_ref, kdf_ref, cumf_ref, of_ref),
            (ub_ref, wb_ref, ab_ref, qgb_ref, kdb_ref, cumb_ref, ob_ref))
    chains = [(direction, hh, slice(hh * dk, (hh + 1) * dk)) for direction in range(2) for hh in range(heads)]
    states = [state_ref[direction, hh] for direction, hh, _ in chains]
    sbs = [state.astype(BF16) for state in states]
    v_news = [(refs[direction][0][:, sl] - _dot(refs[direction][1][:, sl], sb)).astype(BF16)
              for (direction, _, sl), sb in zip(chains, sbs)]
    for (direction, hh, sl), state, sb, v_new in zip(chains, states, sbs, v_news):
        _, _, attn_ref, qg_ref, kd_ref, cum_ref, o_ref = refs[direction]
        o_ref[:, sl] = _dot(attn_ref[:, sl], v_new) + _dot(qg_ref[:, sl], sb)
        last = _lane_pick(cum_ref[hh], c - 1 if direction == 0 else 0)
        state_ref[direction, hh] = state * jnp.exp(last) + _dot_tn(kd_ref[:, sl], v_new)


def _gdn(qkv, gates3, n_heads, beta_row0, cum_row0):
    s = qkv.shape[0]
    dk = GDN_HEAD_DIM
    nc = s // CHUNK
    width = n_heads * dk
    heads = _pick(n_heads, (8, 4, 2, 1))
    wb = heads * dk
    gate4 = gates3.reshape(GATE_ROWS // n_heads, n_heads // heads, heads, 1, s)

    def gate_rows(r0):
        return pl.BlockSpec((2, None, heads, 1, CHUNK), lambda h, c: (r0 // (2 * n_heads), h, 0, 0, c))

    cols = [pl.BlockSpec((CHUNK, wb), lambda h, c, o=o: (c, o * (n_heads // heads) + h)) for o in range(3)]
    out_blk = pl.BlockSpec((2, CHUNK, wb), lambda h, c: (0, c, h))
    u, w, attn, qg, kd = pl.pallas_call(
        functools.partial(_gdn_prep_kernel, heads=heads),
        grid=(n_heads // heads, nc),
        in_specs=cols + [gate_rows(beta_row0), gate_rows(cum_row0)],
        out_specs=[out_blk] * 5,
        out_shape=[jax.ShapeDtypeStruct((2, s, width), dt) for dt in (F32, BF16, BF16, BF16, BF16)],
        compiler_params=_params("parallel", "parallel"),
        name="gdn_prep",
    )(qkv, qkv, qkv, gate4, gate4)


    def specs(direction, chunk):
        seq = pl.BlockSpec((None, CHUNK, wb), lambda h, c: (direction, chunk(c), h))
        cum = pl.BlockSpec((None, None, heads, 1, CHUNK),
                           lambda h, c: (cum_row0 // n_heads + direction, h, 0, 0, chunk(c)))
        return [seq] * 5 + [cum]

    out_f, out_b = _both_directions(lambda chunk: pl.BlockSpec((CHUNK, wb), lambda h, c: (chunk(c), h)), nc)
    ops = (u, w, attn, qg, kd, gate4)
    return pl.pallas_call(
        functools.partial(_gdn_scan_kernel, heads=heads),
        grid=(n_heads // heads, nc),
        in_specs=specs(0, lambda c: c) + specs(1, lambda c: nc - 1 - c),
        out_specs=[out_f, out_b],
        out_shape=[jax.ShapeDtypeStruct((s, width), F32)] * 2,
        scratch_shapes=[pltpu.VMEM((2, heads, dk, dk), F32)],
        compiler_params=_params("parallel", "arbitrary"),
        name="gdn_scan",
    )(*ops, *ops)


def _mlstm_kernel(qf_ref, kf_ref, vf_ref, igf_ref, cumf_ref, qb_ref, kb_ref, vb_ref, igb_ref, cumb_ref,
                  of_ref, ob_ref, mem_ref, nrm_ref, m_ref, *, heads):
    @pl.when(pl.program_id(1) == 0)
    def _():
        mem_ref[...] = jnp.zeros_like(mem_ref)
        nrm_ref[...] = jnp.zeros_like(nrm_ref)
        m_ref[...] = jnp.zeros_like(m_ref)

    c = qf_ref.shape[0]
    dk = MLSTM_HEAD_DIM
    refs = ((qf_ref, kf_ref, vf_ref, igf_ref, cumf_ref, of_ref),
            (qb_ref, kb_ref, vb_ref, igb_ref, cumb_ref, ob_ref))
    chains = []
    for direction, (q_ref, k_ref, v_ref, ig_ref, cum_ref, o_ref) in enumerate(refs):
        incl, _ = _dir_masks(c, direction)
        for hh in range(heads):
            sl = slice(hh * dk, (hh + 1) * dk)
            q = q_ref[:, sl]
            k = k_ref[:, sl] * (dk ** -0.5)
            vb = v_ref[:, sl].astype(BF16)
            qb = q.astype(BF16)
            ig_row = ig_ref[hh]
            cum_row = cum_ref[hh]
            ig = _row_to_col(ig_row)
            cum = _row_to_col(cum_row)
            log_d = jnp.where(incl, cum - cum_row + ig_row, -jnp.inf)
            d_max = jnp.max(log_d, axis=1, keepdims=True)
            qk = _dot_nt(qb, k.astype(BF16))
            b_last = _lane_pick(cum_row, c - 1 if direction == 0 else 0)
            log_w = b_last - cum + ig
            w_max = jnp.max(log_w, axis=0, keepdims=True)
            chains.append((direction, hh, sl, o_ref, q, k, vb, qb, cum, log_d, d_max, qk, b_last, log_w, w_max))
    carried = [(m_ref[direction, hh, 0:1, 0:1], mem_ref[direction, hh], nrm_ref[direction, hh])
               for direction, hh, *_ in chains]
    for (direction, hh, sl, o_ref, q, k, vb, qb, cum, log_d, d_max, qk, *_), (m_prev, mem, nrm) in zip(
            chains, carried):
        m_inter = cum + m_prev
        m_row = jnp.maximum(m_inter, d_max)
        s_mat = qk * jnp.exp(log_d - m_row)
        inter_scale = jnp.exp(m_inter - m_row)
        num = _dot(s_mat.astype(BF16), vb) + inter_scale * _dot(qb, mem.astype(BF16))
        den = (jnp.sum(s_mat, axis=1, keepdims=True)
               + inter_scale * jnp.sum(q * nrm, axis=1, keepdims=True))
        o_ref[:, sl] = num / jnp.maximum(jnp.abs(den), jnp.exp(-m_row))
    for (direction, hh, sl, o_ref, q, k, vb, qb, cum, log_d, d_max, qk, b_last, log_w, w_max), (
            m_prev, mem, nrm) in zip(chains, carried):
        m_new = jnp.maximum(b_last + m_prev, w_max)
        kw = k * jnp.exp(log_w - m_new)
        carry_scale = jnp.exp(b_last + m_prev - m_new)
        mem_ref[direction, hh] = carry_scale * mem + _dot_tn(kw.astype(BF16), vb)
        nrm_ref[direction, hh] = carry_scale * nrm + jnp.sum(kw, axis=0, keepdims=True)
        m_ref[direction, hh] = jnp.broadcast_to(m_new, m_ref.shape[2:])


def _mlstm(p, offs, gates3, n_heads, ig_row0, cum_row0):
    s = p.shape[0]
    dk = MLSTM_HEAD_DIM
    nc = s // CHUNK
    heads = _pick(n_heads, (4, 2, 1))
    wb = heads * dk
    qo, ko, vo = (o // wb for o in offs)
    gate4 = gates3.reshape(GATE_ROWS // n_heads, n_heads // heads, heads, 1, s)

    def specs(direction, chunk):
        col = [pl.BlockSpec((CHUNK, wb), lambda h, c, o=o: (chunk(c), o + h)) for o in (qo, ko, vo)]
        rows = [pl.BlockSpec((None, None, heads, 1, CHUNK),
                             lambda h, c, r0=r0: (r0 // n_heads + direction, h, 0, 0, chunk(c)))
                for r0 in (ig_row0, cum_row0)]
        return col + rows

    out_f, out_b = _both_directions(lambda chunk: pl.BlockSpec((CHUNK, wb), lambda h, c: (chunk(c), h)), nc)
    return pl.pallas_call(
        functools.partial(_mlstm_kernel, heads=heads),
        grid=(n_heads // heads, nc),
        in_specs=specs(0, lambda c: c) + specs(1, lambda c: nc - 1 - c),
        out_specs=[out_f, out_b],
        out_shape=[jax.ShapeDtypeStruct((s, n_heads * dk), F32)] * 2,
        scratch_shapes=[pltpu.VMEM((2, heads, dk, dk), F32), pltpu.VMEM((2, heads, 1, dk), F32),
                        pltpu.VMEM((2, heads, 8, 128), F32)],
        compiler_params=_params("parallel", "arbitrary"),
        name="mlstm_scan",
    )(p, p, p, gate4, gate4, p, p, p, gate4, gate4)


def _finalize_kernel(yf_ref, yb_ref, g_ref, w_ref, o_ref, *, head_dim, subtract_mean, swish_gate):
    y = yf_ref[...] + yb_ref[...]
    g = g_ref[...]
    gate = jax.nn.sigmoid(g)
    if swish_gate:
        gate = g * gate
    w = w_ref[...]
    for hh in range(y.shape[1] // head_dim):
        sl = slice(hh * head_dim, (hh + 1) * head_dim)
        t = y[:, sl]
        if subtract_mean:
            t = t - jnp.mean(t, axis=-1, keepdims=True)
        t = t * lax.rsqrt(jnp.mean(t * t, axis=-1, keepdims=True) + EPS)
        o_ref[:, sl] = (t * w[:, sl] * gate[:, sl]).astype(o_ref.dtype)


def _finalize(y2, p, gate_off, norm_w, head_dim, subtract_mean, swish_gate):
    y_f, y_b = y2
    s, width = y_f.shape
    tm = _pick(s, (512, 256, 128))
    tc = _pick(width, (2048, 1024, 512, 256))
    go = gate_off // tc
    blk = pl.BlockSpec((tm, tc), lambda i, j: (i, j))
    return pl.pallas_call(
        functools.partial(_finalize_kernel, head_dim=head_dim, subtract_mean=subtract_mean,
                          swish_gate=swish_gate),
        grid=(s // tm, width // tc),
        in_specs=[blk, blk, pl.BlockSpec((tm, tc), lambda i, j: (i, go + j)),
                  pl.BlockSpec((1, tc), lambda i, j: (0, j))],
        out_specs=blk,
        out_shape=jax.ShapeDtypeStruct((s, width), BF16),
        compiler_params=_params("parallel", "parallel"),
        name="branch_finalize",
    )(y_f, y_b, p, norm_w.reshape(1, width))


def _merge_kernel(ya_ref, yb_ref, yc_ref, wb_ref, la_ref, lb_ref, lc_ref, o_ref):
    acc = jax.nn.sigmoid(la_ref[...]) * _dot(ya_ref[...], wb_ref[0])
    acc = acc + jax.nn.sigmoid(lb_ref[...]) * _dot(yb_ref[...], wb_ref[1])
    acc = acc + jax.nn.sigmoid(lc_ref[...]) * _dot(yc_ref[...], wb_ref[2])
    o_ref[...] = acc.astype(o_ref.dtype)


def _merge(ys, w_branch, layer, logits, d_model):
    s, width = ys[0].shape
    tm = _pick(s, (512, 256, 128))
    tn = _pick(d_model, (512, 256, 128))
    nb = d_model // tn
    y_blk = pl.BlockSpec((tm, width), lambda i, j: (i, 0))

    def logit(r):
        return pl.BlockSpec((tm, tn), lambda i, j: (i, r * nb + j))

    return pl.pallas_call(
        _merge_kernel,
        grid=(s // tm, nb),
        in_specs=[y_blk, y_blk, y_blk,
                  pl.BlockSpec((None, N_BRANCHES, width, tn), lambda i, j: (layer, 0, 0, j)),
                  logit(0), logit(1), logit(2)],
        out_specs=pl.BlockSpec((tm, tn), lambda i, j: (i, j)),
        out_shape=jax.ShapeDtypeStruct((s, d_model), BF16),
        compiler_params=_params("parallel", "parallel"),
        name="merge",
    )(*ys, w_branch, logits, logits, logits)


def kernel(x, positions, norm_mix, w_in, conv_w, gdn_a_log, gdn_dt_bias, mlstm_i_bias, mlstm_f_bias,
           ret_norm, gdn_norm, mlstm_norm, w_branch, w_out, norm_ffn, w_ffn_in, w_ffn_out, norm_final):
    batch, s, d = x.shape
    assert batch == 1 and s % CHUNK == 0
    depth = w_in.shape[0]
    mix = d // 2
    ret_heads = mix // RET_HEAD_DIM
    gdn_heads = mix // GDN_HEAD_DIM
    mlstm_heads = mix // MLSTM_HEAD_DIM
    d_ff = w_ffn_out.shape[1]
    assert 4 * gdn_heads + 4 * mlstm_heads <= GATE_ROWS and GATE_ROWS % (2 * gdn_heads) == 0

    sizes = (mix, mix, mix, mix, 3 * mix, mix, 2 * gdn_heads, 2 * gdn_heads,
             mix, mix, mix, mix, 2 * mlstm_heads, 2 * mlstm_heads, N_BRANCHES * d)
    starts = np.concatenate([[0], np.cumsum(sizes)]).tolist()
    n_lead = starts[6]
    small = (6, 7, 12, 13)
    n_small = sum(sizes[idx] for idx in small)

    x2 = x.reshape(s, d)
    tables = _rope_table(positions.reshape(s), RET_HEAD_DIM // 2)
    w_in_t = jnp.swapaxes(w_in, 1, 2)
    w_branch_bf = w_branch.astype(BF16)
    w_ffn_out_bf = w_ffn_out.astype(BF16)

    for l in range(depth):
        w_small_t = jnp.concatenate([w_in_t[l, starts[idx]:starts[idx + 1], :] for idx in small], axis=0)
        w_small_t = jnp.pad(w_small_t, ((0, GATE_ROWS - n_small), (0, 0)))
        zeros = jnp.zeros((2 * gdn_heads,), F32)
        gate_bias = jnp.concatenate([zeros, gdn_dt_bias[l].reshape(-1), mlstm_i_bias[l].reshape(-1),
                                     mlstm_f_bias[l].reshape(-1)])
        gate_scale = jnp.concatenate([zeros + 1.0, -jnp.exp(gdn_a_log[l].reshape(-1)),
                                      jnp.ones((4 * mlstm_heads,), F32)])
        gate_par = jnp.pad(jnp.stack([gate_bias, gate_scale], axis=1), ((0, GATE_ROWS - n_small), (0, 0)))

        h = _rmsnorm(x2, norm_mix[l], BF16)
        p_lead = _matmul_wt(h, w_in_t, l, 0, n_lead, "in_proj_lead")
        p_mlstm = _matmul_wt(h, w_in_t, l, starts[8], 4 * mix, "in_proj_mlstm")
        logits = _matmul_wt(h, w_in_t, l, starts[14], N_BRANCHES * d, "in_proj_logits")
        gates3 = _gates(h, w_small_t, gate_par, gdn_heads, mlstm_heads).reshape(GATE_ROWS, 1, s)

        y_ret = _retention(p_lead, (starts[0], starts[1], starts[2]), tables, ret_heads)
        y_ret = _finalize(y_ret, p_lead, starts[3], ret_norm[l], RET_HEAD_DIM, True, True)

        qkv = _gdn_conv(p_lead, starts[4], 3 * mix, conv_w[l])
        y_gdn = _gdn(qkv, gates3, gdn_heads, 0, 2 * gdn_heads)
        y_gdn = _finalize(y_gdn, p_lead, starts[5], gdn_norm[l], GDN_HEAD_DIM, False, True)

        y_ml = _mlstm(p_mlstm, (0, mix, 2 * mix), gates3, mlstm_heads,
                      4 * gdn_heads, 4 * gdn_heads + 2 * mlstm_heads)
        y_ml = _finalize(y_ml, p_mlstm, 3 * mix, mlstm_norm[l], MLSTM_HEAD_DIM, False, False)

        merged = _merge((y_ret, y_gdn, y_ml), w_branch_bf, l, logits, d)
        x2 = _matmul_f32w(merged, w_out, l, d, "out_proj", res=x2)

        h = _rmsnorm(x2, norm_ffn[l], BF16)
        act = _ffn_in(h, w_ffn_in, l, d_ff)
        x2 = _matmul_residual(act, w_ffn_out_bf, l, x2, "ffn_out")

    return _rmsnorm(x2, norm_final, x.dtype).reshape(batch, s, d)
```

```python
import functools

import numpy as np
import jax
import jax.numpy as jnp
from jax import lax
from jax.experimental import pallas as pl
from jax.experimental.pallas import tpu as pltpu

RET_HEAD_DIM = 256
GDN_HEAD_DIM = 128
MLSTM_HEAD_DIM = 256
CONV_WIDTH = 5
ROPE_BASE = 10000.0
N_BRANCHES = 3
EPS = 1e-6

CHUNK = 128
RET_CHUNK = 256
GATE_ROWS = 128
F32_SUBLANES = 8
V7X_VMEM_BYTES = 64 * 1024 * 1024
VMEM_LIMIT = V7X_VMEM_BYTES - 8 * 1024 * 1024
VMEM_LIMIT_WIDE_PANEL = V7X_VMEM_BYTES - 4 * 1024 * 1024

F32 = jnp.float32
BF16 = jnp.bfloat16


def _params(*sem, vmem=VMEM_LIMIT):
    return pltpu.CompilerParams(dimension_semantics=sem, vmem_limit_bytes=vmem)


def _pick(n, candidates):
    for c in candidates:
        if n % c == 0:
            return c
    raise ValueError(f"no tile in {candidates} divides {n}")


def _dot(a, b):
    return jnp.dot(a, b, preferred_element_type=F32)


def _dot_nt(a, b):
    return lax.dot_general(a, b, (((1,), (1,)), ((), ())), preferred_element_type=F32)


def _dot_tn(a, b):
    return lax.dot_general(a, b, (((0,), (0,)), ((), ())), preferred_element_type=F32)


def _dot_f32(a, b):
    return jnp.dot(a, b, preferred_element_type=F32, precision=lax.Precision.HIGHEST)


def _split(a):
    hi = a.astype(BF16)
    return hi, (a - hi.astype(F32)).astype(BF16)


def _dot_split(a, b):
    a_hi, a_lo = _split(a)
    b_hi, b_lo = _split(b)
    return _dot(a_hi, b_hi) + (_dot(a_hi, b_lo) + _dot(a_lo, b_hi))


def _row_to_col(row):
    c = row.shape[-1]
    i = lax.broadcasted_iota(jnp.int32, (c, c), 0)
    j = lax.broadcasted_iota(jnp.int32, (c, c), 1)
    return jnp.sum(jnp.where(i == j, jnp.broadcast_to(row, (c, c)), 0.0), axis=1, keepdims=True)


def _lane_pick(row, idx):
    lane = lax.broadcasted_iota(jnp.int32, row.shape, 1)
    return jnp.sum(jnp.where(lane == idx, row, 0.0), axis=1, keepdims=True)


def _dir_masks(c, direction):
    i = lax.broadcasted_iota(jnp.int32, (c, c), 0)
    j = lax.broadcasted_iota(jnp.int32, (c, c), 1)
    if direction == 0:
        return j <= i, j < i
    return j >= i, j > i


def _rmsnorm_kernel(x_ref, w_ref, o_ref):
    x = x_ref[...]
    y = x * lax.rsqrt(jnp.mean(x * x, axis=-1, keepdims=True) + EPS)
    o_ref[...] = (y * w_ref[...]).astype(o_ref.dtype)


def _rmsnorm(x, w, out_dtype):
    s, d = x.shape
    tm = _pick(s, (512, 256, 128))
    return pl.pallas_call(
        _rmsnorm_kernel,
        grid=(s // tm,),
        in_specs=[pl.BlockSpec((tm, d), lambda i: (i, 0)), pl.BlockSpec((1, d), lambda i: (0, 0))],
        out_specs=pl.BlockSpec((tm, d), lambda i: (i, 0)),
        out_shape=jax.ShapeDtypeStruct((s, d), out_dtype),
        compiler_params=_params("parallel"),
        name="rmsnorm",
    )(x, w.reshape(1, d))


def _mm_cast_res_kernel(a_ref, w_ref, r_ref, o_ref, wb_ref):
    @pl.when(pl.program_id(1) == 0)
    def _():
        wb_ref[...] = w_ref[...].astype(BF16)

    o_ref[...] = r_ref[...] + _dot(a_ref[...], wb_ref[...])


def _matmul_f32w_residual(a, w_stack, layer, res, name):
    m, k = a.shape
    n = w_stack.shape[2]
    tm = _pick(m, (1024, 512, 256, 128))
    tn = _pick(n, (512, 256, 128))
    return pl.pallas_call(
        _mm_cast_res_kernel,
        grid=(n // tn, m // tm),
        in_specs=[pl.BlockSpec((tm, k), lambda j, i: (i, 0)),
                  pl.BlockSpec((None, k, tn), lambda j, i: (layer, 0, j)),
                  pl.BlockSpec((tm, tn), lambda j, i: (i, j))],
        out_specs=pl.BlockSpec((tm, tn), lambda j, i: (i, j)),
        out_shape=jax.ShapeDtypeStruct((m, n), F32),
        scratch_shapes=[pltpu.VMEM((k, tn), BF16)],
        compiler_params=_params("parallel", "arbitrary"),
        name=name,
    )(a, w_stack, res)


def _mm_wt_kernel(a_ref, wt_ref, o_ref, wb_ref):
    @pl.when(pl.program_id(1) == 0)
    def _():
        wb_ref[...] = wt_ref[0].astype(BF16)

    o_ref[...] = _dot_nt(a_ref[...], wb_ref[...]).astype(o_ref.dtype)


def _matmul_wt(a, wt_stack, layer, row0, n, name):
    m, k = a.shape
    tm = _pick(m, (512, 256, 128))
    tn = _pick(n, (1024, 512, 256, 128))
    return pl.pallas_call(
        _mm_wt_kernel,
        grid=(n // tn, m // tm),
        in_specs=[pl.BlockSpec((tm, k), lambda j, i: (i, 0)),
                  pl.BlockSpec((pl.Element(1), pl.Element(tn), pl.Element(k)),
                               lambda j, i: (layer, pl.multiple_of(row0 + j * tn, F32_SUBLANES), 0))],
        out_specs=pl.BlockSpec((tm, tn), lambda j, i: (i, j)),
        out_shape=jax.ShapeDtypeStruct((m, n), F32),
        scratch_shapes=[pltpu.VMEM((tn, k), BF16)],
        compiler_params=_params("parallel", "arbitrary", vmem=VMEM_LIMIT_WIDE_PANEL),
        name=name,
    )(a, wt_stack)


def _mm_res_kernel(a_ref, b_ref, r_ref, o_ref):
    o_ref[...] = r_ref[...] + _dot(a_ref[...], b_ref[...])


def _matmul_residual(a, b_stack, layer, res, name):
    m, k = a.shape
    n = b_stack.shape[2]
    tm = _pick(m, (1024, 512, 256, 128))
    tn = _pick(n, (256, 128))
    return pl.pallas_call(
        _mm_res_kernel,
        grid=(m // tm, n // tn),
        in_specs=[pl.BlockSpec((tm, k), lambda i, j: (i, 0), pipeline_mode=pl.Buffered(1)),
                  pl.BlockSpec((None, k, tn), lambda i, j: (layer, 0, j)),
                  pl.BlockSpec((tm, tn), lambda i, j: (i, j))],
        out_specs=pl.BlockSpec((tm, tn), lambda i, j: (i, j)),
        out_shape=jax.ShapeDtypeStruct((m, n), F32),
        compiler_params=_params("parallel", "arbitrary"),
        name=name,
    )(a, b_stack, res)


def _swiglu_kernel(a_ref, wg_ref, wu_ref, o_ref, wgb_ref, wub_ref):
    @pl.when(pl.program_id(1) == 0)
    def _():
        wgb_ref[...] = wg_ref[...].astype(BF16)
        wub_ref[...] = wu_ref[...].astype(BF16)

    a = a_ref[...]
    gate = _dot(a, wgb_ref[...])
    up = _dot(a, wub_ref[...])
    o_ref[...] = (gate * jax.nn.sigmoid(gate) * up).astype(o_ref.dtype)


def _ffn_in(h, w_stack, layer, d_ff):
    m, k = h.shape
    tm = _pick(m, (1024, 512, 256, 128))
    tn = _pick(d_ff, (256, 128))
    nb = d_ff // tn
    return pl.pallas_call(
        _swiglu_kernel,
        grid=(nb, m // tm),
        in_specs=[pl.BlockSpec((tm, k), lambda j, i: (i, 0)),
                  pl.BlockSpec((None, k, tn), lambda j, i: (layer, 0, j)),
                  pl.BlockSpec((None, k, tn), lambda j, i: (layer, 0, j + nb))],
        out_specs=pl.BlockSpec((tm, tn), lambda j, i: (i, j)),
        out_shape=jax.ShapeDtypeStruct((m, d_ff), BF16),
        scratch_shapes=[pltpu.VMEM((k, tn), BF16), pltpu.VMEM((k, tn), BF16)],
        compiler_params=_params("parallel", "arbitrary"),
        name="ffn_in",
    )(h, w_stack, w_stack)


def _gates_kernel(w_ref, h_ref, par_ref, o_ref, *, n_gdn, n_mlstm):
    z = _dot_nt(w_ref[...].astype(BF16), h_ref[...])
    r, tm = z.shape
    bias = par_ref[:, 0:1]
    scale = par_ref[:, 1:2]
    row = lax.broadcasted_iota(jnp.int32, (r, tm), 0)
    zb = z + bias
    soft = jnp.log1p(jnp.exp(-jnp.abs(zb)))
    softplus = jnp.maximum(zb, 0.0) + soft
    log_sig = jnp.minimum(zb, 0.0) - soft
    b0, b1, b2, b3 = 2 * n_gdn, 4 * n_gdn, 4 * n_gdn + 2 * n_mlstm, 4 * n_gdn + 4 * n_mlstm
    is_alpha = (row >= b0) & (row < b1)
    is_logf = (row >= b2) & (row < b3)
    raw = jnp.where(row < b0, jax.nn.sigmoid(zb),
                    jnp.where(is_alpha, scale * softplus,
                              jnp.where(row < b2, zb, jnp.where(is_logf, log_sig, 0.0))))
    pj = lax.broadcasted_iota(jnp.int32, (tm, tm), 0)
    pi = lax.broadcasted_iota(jnp.int32, (tm, tm), 1)
    same = (pj // CHUNK) == (pi // CHUNK)
    prefix = jnp.where(same & (pj <= pi), 1.0, 0.0)
    suffix = jnp.where(same & (pj >= pi), 1.0, 0.0)
    cum_p = _dot_f32(raw, prefix)
    cum_s = _dot_f32(raw, suffix)
    dir1 = (is_alpha & (row >= b0 + n_gdn)) | (is_logf & (row >= b2 + n_mlstm))
    cum = jnp.where(dir1, cum_s, cum_p)
    o_ref[...] = jnp.where(is_alpha | is_logf, cum, raw)


def _gates(h, w, par, n_gdn, n_mlstm):
    s, d = h.shape
    tm = _pick(s, (256, 128))
    return pl.pallas_call(
        functools.partial(_gates_kernel, n_gdn=n_gdn, n_mlstm=n_mlstm),
        grid=(s // tm,),
        in_specs=[pl.BlockSpec((GATE_ROWS, d), lambda i: (0, 0)),
                  pl.BlockSpec((tm, d), lambda i: (i, 0)),
                  pl.BlockSpec((GATE_ROWS, 2), lambda i: (0, 0))],
        out_specs=pl.BlockSpec((GATE_ROWS, tm), lambda i: (0, i)),
        out_shape=jax.ShapeDtypeStruct((GATE_ROWS, s), F32),
        compiler_params=_params("parallel"),
        name="gates",
    )(w, h, par)


def _rope_kernel(pos_ref, freq_ref, cos_ref, sin_ref):
    ang = pos_ref[...].astype(F32) * freq_ref[...]
    cos_ref[...] = jnp.cos(ang)
    sin_ref[...] = jnp.sin(ang)


def _rope_table(positions, half):
    s = positions.shape[0]
    tm = _pick(s, (512, 256, 128))
    inv_freq = (ROPE_BASE ** (-jnp.arange(half, dtype=F32) / half)).reshape(1, half)
    return pl.pallas_call(
        _rope_kernel,
        grid=(s // tm,),
        in_specs=[pl.BlockSpec((tm, 1), lambda i: (i, 0)), pl.BlockSpec((1, half), lambda i: (0, 0))],
        out_specs=[pl.BlockSpec((tm, half), lambda i: (i, 0))] * 2,
        out_shape=[jax.ShapeDtypeStruct((s, half), F32)] * 2,
        compiler_params=_params("parallel"),
        name="rope_table",
    )(positions.reshape(s, 1), inv_freq)


def _both_directions(make_spec, nc):
    return [make_spec(lambda c: c), make_spec(lambda c: nc - 1 - c)]


def _retention_kernel(lg_ref, qf_ref, kf_ref, vf_ref, cosf_ref, sinf_ref,
                      qb_ref, kb_ref, vb_ref, cosb_ref, sinb_ref, of_ref, ob_ref, state_ref, decay_ref,
                      *, heads):
    hg = pl.program_id(0)
    c = qf_ref.shape[0]
    dk = RET_HEAD_DIM
    half = dk // 2

    @pl.when(pl.program_id(1) == 0)
    def _():
        state_ref[...] = jnp.zeros_like(state_ref)
        i = lax.broadcasted_iota(jnp.int32, (c, c), 0)
        j = lax.broadcasted_iota(jnp.int32, (c, c), 1)
        dist = jnp.abs(i - j).astype(F32)
        for direction in range(2):
            mask = (j <= i) if direction == 0 else (j > i)
            for hh in range(heads):
                lg = jnp.full((1, 1), lg_ref[hg * heads + hh], F32)
                decay_ref[direction, hh] = jnp.where(mask, jnp.exp(lg * jnp.where(mask, dist, 0.0)), 0.0)

    idx = lax.broadcasted_iota(jnp.int32, (c, 1), 0).astype(F32)
    refs = ((qf_ref, kf_ref, vf_ref, cosf_ref, sinf_ref, of_ref),
            (qb_ref, kb_ref, vb_ref, cosb_ref, sinb_ref, ob_ref))
    chains = []
    for direction, (q_ref, k_ref, v_ref, cos_ref, sin_ref, o_ref) in enumerate(refs):
        cos = cos_ref[...]
        sin = sin_ref[...]

        def rot(t, cos=cos, sin=sin):
            t1, t2 = t[:, :half], t[:, half:]
            return jnp.concatenate([t1 * cos - t2 * sin, t1 * sin + t2 * cos], axis=1)

        for hh in range(heads):
            sl = slice(hh * dk, (hh + 1) * dk)
            lg = jnp.full((1, 1), lg_ref[hg * heads + hh], F32)
            decay_intra = decay_ref[direction, hh]
            q_decay = jnp.exp(lg * ((idx + 1.0) if direction == 0 else (c - idx)))
            k_decay = jnp.exp(lg * ((c - 1.0 - idx) if direction == 0 else idx))
            q = rot(q_ref[:, sl])
            k = rot(k_ref[:, sl]) * (dk ** -0.5)
            v = v_ref[:, sl].astype(BF16)
            scores = (_dot_nt(q.astype(BF16), k.astype(BF16)) * decay_intra).astype(BF16)
            chains.append((direction, hh, sl, o_ref, scores, v, (q * q_decay).astype(BF16),
                           (k * k_decay).astype(BF16), jnp.exp(lg * c)))
    states = [state_ref[direction, hh] for direction, hh, *_ in chains]
    for (direction, hh, sl, o_ref, scores, v, qd, kd, _), state in zip(chains, states):
        o_ref[:, sl] = _dot(scores, v) + _dot(qd, state.astype(BF16))
    for (direction, hh, sl, o_ref, scores, v, qd, kd, chunk_decay), state in zip(chains, states):
        state_ref[direction, hh] = state * chunk_decay + _dot_tn(kd, v)


def _retention(p, offs, tables, n_heads):
    s = p.shape[0]
    dk = RET_HEAD_DIM
    ck = _pick(s, (RET_CHUNK, CHUNK))
    nc = s // ck
    heads = _pick(n_heads, (4, 2, 1))
    wb = heads * dk
    cos, sin = tables
    qo, ko, vo = (o // wb for o in offs)
    log_gamma = jnp.log1p(-jnp.exp2(-5.0 - jnp.arange(n_heads, dtype=F32)))

    def specs(chunk):
        col = [pl.BlockSpec((ck, wb), lambda h, c, o=o: (chunk(c), o + h)) for o in (qo, ko, vo)]
        tab = pl.BlockSpec((ck, dk // 2), lambda h, c: (chunk(c), 0))
        return col + [tab, tab]

    fwd, bwd = _both_directions(specs, nc)
    out_f, out_b = _both_directions(lambda chunk: pl.BlockSpec((ck, wb), lambda h, c: (chunk(c), h)), nc)
    return pl.pallas_call(
        functools.partial(_retention_kernel, heads=heads),
        grid=(n_heads // heads, nc),
        in_specs=[pl.BlockSpec(memory_space=pltpu.SMEM)] + fwd + bwd,
        out_specs=[out_f, out_b],
        out_shape=[jax.ShapeDtypeStruct((s, n_heads * dk), F32)] * 2,
        scratch_shapes=[pltpu.VMEM((2, heads, dk, dk), F32), pltpu.VMEM((2, heads, ck, ck), F32)],
        compiler_params=_params("parallel", "arbitrary"),
        name="retention_scan",
    )(log_gamma, p, p, p, cos, sin, p, p, p, cos, sin)


def _conv_kernel(prev_ref, cur_ref, next_ref, w_ref, o_ref, pad_ref, *, n_qk_blocks, n_q_blocks, halo):
    i = pl.program_id(0)
    jb = pl.program_id(1)
    tr, tc = cur_ref.shape
    keep_prev = jnp.where(i == 0, 0.0, 1.0)
    keep_next = jnp.where(i == pl.num_programs(0) - 1, 0.0, 1.0)
    pad_ref[0:halo, :] = prev_ref[...] * keep_prev
    pad_ref[halo:halo + tr, :] = cur_ref[...]
    pad_ref[halo + tr:, :] = next_ref[...] * keep_next
    side = (CONV_WIDTH - 1) // 2
    padded = pad_ref[...]
    rows = tr + 2 * halo
    acc = jnp.zeros((tr, tc), F32)
    for tap in range(CONV_WIDTH):
        back = side - tap
        shifted = padded if back == 0 else pltpu.roll(padded, back % rows, 0)
        acc = acc + shifted[halo:halo + tr] * w_ref[tap:tap + 1, :]
    y = acc * jax.nn.sigmoid(acc)
    is_qk = jb < n_qk_blocks
    q_scale = jnp.where(jb < n_q_blocks, GDN_HEAD_DIM ** -0.5, 1.0)
    for g in range(tc // GDN_HEAD_DIM):
        sl = slice(g * GDN_HEAD_DIM, (g + 1) * GDN_HEAD_DIM)
        t = y[:, sl]
        normed = t * lax.rsqrt(jnp.sum(t * t, axis=-1, keepdims=True) + EPS) * q_scale
        o_ref[:, sl] = jnp.where(is_qk, normed, t)


def _gdn_conv(p, off, width, conv_w):
    s = p.shape[0]
    halo = 8
    tr = _pick(s, (1024, 512, 256, 128))
    tc = _pick(width // 3, (512, 256, 128))
    ob = off // tc
    n_row8 = s // halo
    return pl.pallas_call(
        functools.partial(_conv_kernel, n_qk_blocks=2 * (width // 3) // tc, n_q_blocks=(width // 3) // tc,
                          halo=halo),
        grid=(s // tr, width // tc),
        in_specs=[pl.BlockSpec((halo, tc), lambda i, j: (jnp.maximum(i * (tr // halo) - 1, 0), ob + j)),
                  pl.BlockSpec((tr, tc), lambda i, j: (i, ob + j)),
                  pl.BlockSpec((halo, tc), lambda i, j: (jnp.minimum((i + 1) * (tr // halo), n_row8 - 1), ob + j)),
                  pl.BlockSpec((CONV_WIDTH, tc), lambda i, j: (0, j))],
        out_specs=pl.BlockSpec((tr, tc), lambda i, j: (i, j)),
        out_shape=jax.ShapeDtypeStruct((s, width), F32),
        scratch_shapes=[pltpu.VMEM((tr + 2 * halo, tc), F32)],
        compiler_params=_params("parallel", "parallel"),
        name="gdn_conv",
    )(p, p, p, conv_w)


def _unit_triangular_inverses(lms):
    c = lms[0].shape[0]
    i = lax.broadcasted_iota(jnp.int32, (c, c), 0)
    j = lax.broadcasted_iota(jnp.int32, (c, c), 1)
    level = jnp.bitwise_xor(i, j)
    eye = jnp.where(i == j, 1.0, 0.0)
    xs = [eye - jnp.where(level == 1, lm, 0.0) for lm in lms]
    span = 2
    while span < c:
        joins = (level >= span) & (level < 2 * span)
        ys = [_dot(jnp.where(joins, lm, 0.0).astype(BF16), x.astype(BF16)) for lm, x in zip(lms, xs)]
        xs = [x - _dot(x.astype(BF16), y.astype(BF16)) for x, y in zip(xs, ys)]
        span *= 2
    resids = [eye - x - _dot_split(lm, x) for lm, x in zip(lms, xs)]
    return [x + _dot(x.astype(BF16), r.astype(BF16)) for x, r in zip(xs, resids)]


def _gdn_prep_kernel(q_ref, k_ref, v_ref, beta_ref, cum_ref, u_ref, w_ref, attn_ref, qg_ref, kd_ref,
                     *, heads):
    c = q_ref.shape[0]
    dk = GDN_HEAD_DIM
    problems = [(direction, hh) for direction in range(2) for hh in range(heads)]
    masks = [_dir_masks(c, direction) for direction in range(2)]
    ks = [k_ref[:, hh * dk:(hh + 1) * dk] for hh in range(heads)]
    kbs = [k.astype(BF16) for k in ks]
    cum_rows = [cum_ref[direction, hh] for direction, hh in problems]
    cums = [_row_to_col(row) for row in cum_rows]
    betas = [_row_to_col(beta_ref[direction, hh]) for direction, hh in problems]
    decays = []
    for (direction, _), cum, row in zip(problems, cums, cum_rows):
        incl = masks[direction][0]
        decays.append(jnp.where(incl, jnp.exp(jnp.where(incl, cum - row, 0.0)), 0.0))
    k_betas = [ks[hh] * beta for (_, hh), beta in zip(problems, betas)]
    lowers = [_dot_nt(k_beta.astype(BF16), kbs[hh]) * jnp.where(masks[direction][1], decay, 0.0)
              for (direction, hh), k_beta, decay in zip(problems, k_betas, decays)]
    invs = _unit_triangular_inverses(lowers)
    e_cums = [jnp.exp(cum) for cum in cums]
    us = [_dot_split(inv, v_ref[:, hh * dk:(hh + 1) * dk] * beta)
          for (_, hh), inv, beta in zip(problems, invs, betas)]
    ws = [_dot_split(inv, k_beta * e_cum) for inv, k_beta, e_cum in zip(invs, k_betas, e_cums)]
    for n, (direction, hh) in enumerate(problems):
        sl = slice(hh * dk, (hh + 1) * dk)
        q = q_ref[:, sl]
        u_ref[direction, :, sl] = us[n]
        w_ref[direction, :, sl] = ws[n].astype(w_ref.dtype)
        attn_ref[direction, :, sl] = (_dot_nt(q.astype(BF16), kbs[hh]) * decays[n]).astype(attn_ref.dtype)
        qg_ref[direction, :, sl] = (q * e_cums[n]).astype(qg_ref.dtype)
        last = _lane_pick(cum_rows[n], c - 1 if direction == 0 else 0)
        kd_ref[direction, :, sl] = (ks[hh] * jnp.exp(last - cums[n])).astype(kd_ref.dtype)


def _gdn_scan_kernel(uf_ref, wf_ref, af_ref, qgf_ref, kdf_ref, cumf_ref,
                     ub_ref, wb_ref, ab_ref, qgb_ref, kdb_ref, cumb_ref, of_ref, ob_ref, state_ref,
                     *, heads):
    @pl.when(pl.program_id(1) == 0)
    def _():
        state_ref[...] = jnp.zeros_like(state_ref)

    c = uf_ref.shape[0]
    dk = GDN_HEAD_DIM
    refs = ((uf_ref, wf_ref, af_ref, qgf_ref, kdf_ref, cumf_ref, of_ref),
            (ub_ref, wb_ref, ab_ref, qgb_ref, kdb_ref, cumb_ref, ob_ref))
    chains = [(direction, hh, slice(hh * dk, (hh + 1) * dk)) for direction in range(2) for hh in range(heads)]
    states = [state_ref[direction, hh] for direction, hh, _ in chains]
    sbs = [state.astype(BF16) for state in states]
    v_news = [(refs[direction][0][:, sl] - _dot(refs[direction][1][:, sl], sb)).astype(BF16)
              for (direction, _, sl), sb in zip(chains, sbs)]
    for (direction, hh, sl), state, sb, v_new in zip(chains, states, sbs, v_news):
        _, _, attn_ref, qg_ref, kd_ref, cum_ref, o_ref = refs[direction]
        o_ref[:, sl] = _dot(attn_ref[:, sl], v_new) + _dot(qg_ref[:, sl], sb)
        last = _lane_pick(cum_ref[hh], c - 1 if direction == 0 else 0)
        state_ref[direction, hh] = state * jnp.exp(last) + _dot_tn(kd_ref[:, sl], v_new)


def _gdn(qkv, gates3, n_heads, beta_row0, cum_row0):
    s = qkv.shape[0]
    dk = GDN_HEAD_DIM
    nc = s // CHUNK
    width = n_heads * dk
    heads = _pick(n_heads, (8, 4, 2, 1))
    wb = heads * dk
    gate4 = gates3.reshape(GATE_ROWS // n_heads, n_heads // heads, heads, 1, s)

    def gate_rows(r0):
        return pl.BlockSpec((2, None, heads, 1, CHUNK), lambda h, c: (r0 // (2 * n_heads), h, 0, 0, c))

    cols = [pl.BlockSpec((CHUNK, wb), lambda h, c, o=o: (c, o * (n_heads // heads) + h)) for o in range(3)]
    out_blk = pl.BlockSpec((2, CHUNK, wb), lambda h, c: (0, c, h))
    u, w, attn, qg, kd = pl.pallas_call(
        functools.partial(_gdn_prep_kernel, heads=heads),
        grid=(n_heads // heads, nc),
        in_specs=cols + [gate_rows(beta_row0), gate_rows(cum_row0)],
        out_specs=[out_blk] * 5,
        out_shape=[jax.ShapeDtypeStruct((2, s, width), dt) for dt in (F32, BF16, BF16, BF16, BF16)],
        compiler_params=_params("parallel", "parallel"),
        name="gdn_prep",
    )(qkv, qkv, qkv, gate4, gate4)


    def specs(direction, chunk):
        seq = pl.BlockSpec((None, CHUNK, wb), lambda h, c: (direction, chunk(c), h))
        cum = pl.BlockSpec((None, None, heads, 1, CHUNK),
                           lambda h, c: (cum_row0 // n_heads + direction, h, 0, 0, chunk(c)))
        return [seq] * 5 + [cum]

    out_f, out_b = _both_directions(lambda chunk: pl.BlockSpec((CHUNK, wb), lambda h, c: (chunk(c), h)), nc)
    ops = (u, w, attn, qg, kd, gate4)
    return pl.pallas_call(
        functools.partial(_gdn_scan_kernel, heads=heads),
        grid=(n_heads // heads, nc),
        in_specs=specs(0, lambda c: c) + specs(1, lambda c: nc - 1 - c),
        out_specs=[out_f, out_b],
        out_shape=[jax.ShapeDtypeStruct((s, width), F32)] * 2,
        scratch_shapes=[pltpu.VMEM((2, heads, dk, dk), F32)],
        compiler_params=_params("parallel", "arbitrary"),
        name="gdn_scan",
    )(*ops, *ops)


def _mlstm_kernel(qf_ref, kf_ref, vf_ref, igf_ref, cumf_ref, qb_ref, kb_ref, vb_ref, igb_ref, cumb_ref,
                  of_ref, ob_ref, mem_ref, nrm_ref, m_ref, *, heads):
    @pl.when(pl.program_id(1) == 0)
    def _():
        mem_ref[...] = jnp.zeros_like(mem_ref)
        nrm_ref[...] = jnp.zeros_like(nrm_ref)
        m_ref[...] = jnp.zeros_like(m_ref)

    c = qf_ref.shape[0]
    dk = MLSTM_HEAD_DIM
    refs = ((qf_ref, kf_ref, vf_ref, igf_ref, cumf_ref, of_ref),
            (qb_ref, kb_ref, vb_ref, igb_ref, cumb_ref, ob_ref))
    chains = []
    for direction, (q_ref, k_ref, v_ref, ig_ref, cum_ref, o_ref) in enumerate(refs):
        incl, _ = _dir_masks(c, direction)
        for hh in range(heads):
            sl = slice(hh * dk, (hh + 1) * dk)
            q = q_ref[:, sl]
            k = k_ref[:, sl] * (dk ** -0.5)
            vb = v_ref[:, sl].astype(BF16)
            qb = q.astype(BF16)
            ig_row = ig_ref[hh]
            cum_row = cum_ref[hh]
            ig = _row_to_col(ig_row)
            cum = _row_to_col(cum_row)
            log_d = jnp.where(incl, cum - cum_row + ig_row, -jnp.inf)
            d_max = jnp.max(log_d, axis=1, keepdims=True)
            qk = _dot_nt(qb, k.astype(BF16))
            b_last = _lane_pick(cum_row, c - 1 if direction == 0 else 0)
            log_w = b_last - cum + ig
            w_max = jnp.max(log_w, axis=0, keepdims=True)
            chains.append((direction, hh, sl, o_ref, q, k, vb, qb, cum, log_d, d_max, qk, b_last, log_w, w_max))
    carried = [(m_ref[direction, hh, 0:1, 0:1], mem_ref[direction, hh], nrm_ref[direction, hh])
               for direction, hh, *_ in chains]
    for (direction, hh, sl, o_ref, q, k, vb, qb, cum, log_d, d_max, qk, *_), (m_prev, mem, nrm) in zip(
            chains, carried):
        m_inter = cum + m_prev
        m_row = jnp.maximum(m_inter, d_max)
        s_mat = qk * jnp.exp(log_d - m_row)
        inter_scale = jnp.exp(m_inter - m_row)
        num = _dot(s_mat.astype(BF16), vb) + inter_scale * _dot(qb, mem.astype(BF16))
        den = (jnp.sum(s_mat, axis=1, keepdims=True)
               + inter_scale * jnp.sum(q * nrm, axis=1, keepdims=True))
        o_ref[:, sl] = num * (1.0 / jnp.maximum(jnp.abs(den), jnp.exp(-m_row)))
    for (direction, hh, sl, o_ref, q, k, vb, qb, cum, log_d, d_max, qk, b_last, log_w, w_max), (
            m_prev, mem, nrm) in zip(chains, carried):
        m_new = jnp.maximum(b_last + m_prev, w_max)
        kw = k * jnp.exp(log_w - m_new)
        carry_scale = jnp.exp(b_last + m_prev - m_new)
        mem_ref[direction, hh] = carry_scale * mem + _dot_tn(kw.astype(BF16), vb)
        nrm_ref[direction, hh] = carry_scale * nrm + jnp.sum(kw, axis=0, keepdims=True)
        m_ref[direction, hh] = jnp.broadcast_to(m_new, m_ref.shape[2:])


def _mlstm(p, offs, gates3, n_heads, ig_row0, cum_row0):
    s = p.shape[0]
    dk = MLSTM_HEAD_DIM
    nc = s // CHUNK
    heads = _pick(n_heads, (4, 2, 1))
    wb = heads * dk
    qo, ko, vo = (o // wb for o in offs)
    gate4 = gates3.reshape(GATE_ROWS // n_heads, n_heads // heads, heads, 1, s)

    def specs(direction, chunk):
        col = [pl.BlockSpec((CHUNK, wb), lambda h, c, o=o: (chunk(c), o + h)) for o in (qo, ko, vo)]
        rows = [pl.BlockSpec((None, None, heads, 1, CHUNK),
                             lambda h, c, r0=r0: (r0 // n_heads + direction, h, 0, 0, chunk(c)))
                for r0 in (ig_row0, cum_row0)]
        return col + rows

    out_f, out_b = _both_directions(lambda chunk: pl.BlockSpec((CHUNK, wb), lambda h, c: (chunk(c), h)), nc)
    return pl.pallas_call(
        functools.partial(_mlstm_kernel, heads=heads),
        grid=(n_heads // heads, nc),
        in_specs=specs(0, lambda c: c) + specs(1, lambda c: nc - 1 - c),
        out_specs=[out_f, out_b],
        out_shape=[jax.ShapeDtypeStruct((s, n_heads * dk), F32)] * 2,
        scratch_shapes=[pltpu.VMEM((2, heads, dk, dk), F32), pltpu.VMEM((2, heads, 1, dk), F32),
                        pltpu.VMEM((2, heads, 8, 128), F32)],
        compiler_params=_params("parallel", "arbitrary"),
        name="mlstm_scan",
    )(p, p, p, gate4, gate4, p, p, p, gate4, gate4)


def _finalize_kernel(yf_ref, yb_ref, g_ref, w_ref, o_ref, *, head_dim, subtract_mean, swish_gate):
    y = yf_ref[...] + yb_ref[...]
    g = g_ref[...]
    gate = jax.nn.sigmoid(g)
    if swish_gate:
        gate = g * gate
    w = w_ref[...]
    for hh in range(y.shape[1] // head_dim):
        sl = slice(hh * head_dim, (hh + 1) * head_dim)
        t = y[:, sl]
        if subtract_mean:
            t = t - jnp.mean(t, axis=-1, keepdims=True)
        t = t * lax.rsqrt(jnp.mean(t * t, axis=-1, keepdims=True) + EPS)
        o_ref[:, sl] = (t * w[:, sl] * gate[:, sl]).astype(o_ref.dtype)


def _finalize(y2, p, gate_off, norm_w, head_dim, subtract_mean, swish_gate):
    y_f, y_b = y2
    s, width = y_f.shape
    tm = _pick(s, (512, 256, 128))
    tc = _pick(width, (2048, 1024, 512, 256))
    go = gate_off // tc
    blk = pl.BlockSpec((tm, tc), lambda i, j: (i, j))
    return pl.pallas_call(
        functools.partial(_finalize_kernel, head_dim=head_dim, subtract_mean=subtract_mean,
                          swish_gate=swish_gate),
        grid=(s // tm, width // tc),
        in_specs=[blk, blk, pl.BlockSpec((tm, tc), lambda i, j: (i, go + j)),
                  pl.BlockSpec((1, tc), lambda i, j: (0, j))],
        out_specs=blk,
        out_shape=jax.ShapeDtypeStruct((s, width), BF16),
        compiler_params=_params("parallel", "parallel"),
        name="branch_finalize",
    )(y_f, y_b, p, norm_w.reshape(1, width))


def _merge_kernel(ya_ref, yb_ref, yc_ref, wb_ref, la_ref, lb_ref, lc_ref, o_ref):
    acc = jax.nn.sigmoid(la_ref[...]) * _dot(ya_ref[...], wb_ref[0])
    acc = acc + jax.nn.sigmoid(lb_ref[...]) * _dot(yb_ref[...], wb_ref[1])
    acc = acc + jax.nn.sigmoid(lc_ref[...]) * _dot(yc_ref[...], wb_ref[2])
    o_ref[...] = acc.astype(o_ref.dtype)


def _merge(ys, w_branch, layer, logits, d_model):
    s, width = ys[0].shape
    tm = _pick(s, (512, 256, 128))
    tn = _pick(d_model, (512, 256, 128))
    nb = d_model // tn
    y_blk = pl.BlockSpec((tm, width), lambda i, j: (i, 0))

    def logit(r):
        return pl.BlockSpec((tm, tn), lambda i, j: (i, r * nb + j))

    return pl.pallas_call(
        _merge_kernel,
        grid=(s // tm, nb),
        in_specs=[y_blk, y_blk, y_blk,
                  pl.BlockSpec((None, N_BRANCHES, width, tn), lambda i, j: (layer, 0, 0, j)),
                  logit(0), logit(1), logit(2)],
        out_specs=pl.BlockSpec((tm, tn), lambda i, j: (i, j)),
        out_shape=jax.ShapeDtypeStruct((s, d_model), BF16),
        compiler_params=_params("parallel", "parallel"),
        name="merge",
    )(*ys, w_branch, logits, logits, logits)


def kernel(x, positions, norm_mix, w_in, conv_w, gdn_a_log, gdn_dt_bias, mlstm_i_bias, mlstm_f_bias,
           ret_norm, gdn_norm, mlstm_norm, w_branch, w_out, norm_ffn, w_ffn_in, w_ffn_out, norm_final):
    batch, s, d = x.shape
    assert batch == 1 and s % CHUNK == 0
    depth = w_in.shape[0]
    mix = d // 2
    ret_heads = mix // RET_HEAD_DIM
    gdn_heads = mix // GDN_HEAD_DIM
    mlstm_heads = mix // MLSTM_HEAD_DIM
    d_ff = w_ffn_out.shape[1]
    assert 4 * gdn_heads + 4 * mlstm_heads <= GATE_ROWS and GATE_ROWS % (2 * gdn_heads) == 0

    sizes = (mix, mix, mix, mix, 3 * mix, mix, 2 * gdn_heads, 2 * gdn_heads,
             mix, mix, mix, mix, 2 * mlstm_heads, 2 * mlstm_heads, N_BRANCHES * d)
    starts = np.concatenate([[0], np.cumsum(sizes)]).tolist()
    n_lead = starts[6]
    small = (6, 7, 12, 13)
    n_small = sum(sizes[idx] for idx in small)

    x2 = x.reshape(s, d)
    tables = _rope_table(positions.reshape(s), RET_HEAD_DIM // 2)
    w_in_t = jnp.swapaxes(w_in, 1, 2)
    w_branch_bf = w_branch.astype(BF16)
    w_ffn_out_bf = w_ffn_out.astype(BF16)

    for l in range(depth):
        w_small_t = jnp.concatenate([w_in_t[l, starts[idx]:starts[idx + 1], :] for idx in small], axis=0)
        w_small_t = jnp.pad(w_small_t, ((0, GATE_ROWS - n_small), (0, 0)))
        zeros = jnp.zeros((2 * gdn_heads,), F32)
        gate_bias = jnp.concatenate([zeros, gdn_dt_bias[l].reshape(-1), mlstm_i_bias[l].reshape(-1),
                                     mlstm_f_bias[l].reshape(-1)])
        gate_scale = jnp.concatenate([zeros + 1.0, -jnp.exp(gdn_a_log[l].reshape(-1)),
                                      jnp.ones((4 * mlstm_heads,), F32)])
        gate_par = jnp.pad(jnp.stack([gate_bias, gate_scale], axis=1), ((0, GATE_ROWS - n_small), (0, 0)))

        h = _rmsnorm(x2, norm_mix[l], BF16)
        p_lead = _matmul_wt(h, w_in_t, l, 0, n_lead, "in_proj_lead")
        p_mlstm = _matmul_wt(h, w_in_t, l, starts[8], 4 * mix, "in_proj_mlstm")
        logits = _matmul_wt(h, w_in_t, l, starts[14], N_BRANCHES * d, "in_proj_logits")
        gates3 = _gates(h, w_small_t, gate_par, gdn_heads, mlstm_heads).reshape(GATE_ROWS, 1, s)

        y_ret = _retention(p_lead, (starts[0], starts[1], starts[2]), tables, ret_heads)
        y_ret = _finalize(y_ret, p_lead, starts[3], ret_norm[l], RET_HEAD_DIM, True, True)

        qkv = _gdn_conv(p_lead, starts[4], 3 * mix, conv_w[l])
        y_gdn = _gdn(qkv, gates3, gdn_heads, 0, 2 * gdn_heads)
        y_gdn = _finalize(y_gdn, p_lead, starts[5], gdn_norm[l], GDN_HEAD_DIM, False, True)

        y_ml = _mlstm(p_mlstm, (0, mix, 2 * mix), gates3, mlstm_heads,
                      4 * gdn_heads, 4 * gdn_heads + 2 * mlstm_heads)
        y_ml = _finalize(y_ml, p_mlstm, 3 * mix, mlstm_norm[l], MLSTM_HEAD_DIM, False, False)

        merged = _merge((y_ret, y_gdn, y_ml), w_branch_bf, l, logits, d)
        x2 = _matmul_f32w_residual(merged, w_out, l, x2, "out_proj")

        h = _rmsnorm(x2, norm_ffn[l], BF16)
        act = _ffn_in(h, w_ffn_in, l, d_ff)
        x2 = _matmul_residual(act, w_ffn_out_bf, l, x2, "ffn_out")

    return _rmsnorm(x2, norm_final, x.dtype).reshape(batch, s, d)
```

```python
import functools

import numpy as np
import jax
import jax.numpy as jnp
from jax import lax
from jax.experimental import pallas as pl
from jax.experimental.pallas import tpu as pltpu

RET_HEAD_DIM = 256
GDN_HEAD_DIM = 128
MLSTM_HEAD_DIM = 256
CONV_WIDTH = 5
ROPE_BASE = 10000.0
N_BRANCHES = 3
EPS = 1e-6

CHUNK = 128
RET_CHUNK = 256
GATE_ROWS = 128
F32_SUBLANES = 8
V7X_VMEM_BYTES = 64 * 1024 * 1024
VMEM_LIMIT = V7X_VMEM_BYTES - 8 * 1024 * 1024
VMEM_LIMIT_WIDE_PANEL = V7X_VMEM_BYTES - 4 * 1024 * 1024

F32 = jnp.float32
BF16 = jnp.bfloat16


def _params(*sem, vmem=VMEM_LIMIT):
    return pltpu.CompilerParams(dimension_semantics=sem, vmem_limit_bytes=vmem)


def _pick(n, candidates):
    for c in candidates:
        if n % c == 0:
            return c
    raise ValueError(f"no tile in {candidates} divides {n}")


def _dot(a, b):
    return jnp.dot(a, b, preferred_element_type=F32)


def _dot_nt(a, b):
    return lax.dot_general(a, b, (((1,), (1,)), ((), ())), preferred_element_type=F32)


def _dot_tn(a, b):
    return lax.dot_general(a, b, (((0,), (0,)), ((), ())), preferred_element_type=F32)


def _dot_f32(a, b):
    return jnp.dot(a, b, preferred_element_type=F32, precision=lax.Precision.HIGHEST)


def _split(a):
    hi = a.astype(BF16)
    return hi, (a - hi.astype(F32)).astype(BF16)


def _dot_split(a, b):
    a_hi, a_lo = _split(a)
    b_hi, b_lo = _split(b)
    return _dot(a_hi, b_hi) + (_dot(a_hi, b_lo) + _dot(a_lo, b_hi))


def _row_to_col(row):
    c = row.shape[-1]
    i = lax.broadcasted_iota(jnp.int32, (c, c), 0)
    j = lax.broadcasted_iota(jnp.int32, (c, c), 1)
    return jnp.sum(jnp.where(i == j, jnp.broadcast_to(row, (c, c)), 0.0), axis=1, keepdims=True)


def _lane_pick(row, idx):
    lane = lax.broadcasted_iota(jnp.int32, row.shape, 1)
    return jnp.sum(jnp.where(lane == idx, row, 0.0), axis=1, keepdims=True)


def _dir_masks(c, direction):
    i = lax.broadcasted_iota(jnp.int32, (c, c), 0)
    j = lax.broadcasted_iota(jnp.int32, (c, c), 1)
    if direction == 0:
        return j <= i, j < i
    return j >= i, j > i


def _rmsnorm_kernel(x_ref, w_ref, o_ref):
    x = x_ref[...]
    y = x * lax.rsqrt(jnp.mean(x * x, axis=-1, keepdims=True) + EPS)
    o_ref[...] = (y * w_ref[...]).astype(o_ref.dtype)


def _rmsnorm(x, w, out_dtype):
    s, d = x.shape
    tm = _pick(s, (512, 256, 128))
    return pl.pallas_call(
        _rmsnorm_kernel,
        grid=(s // tm,),
        in_specs=[pl.BlockSpec((tm, d), lambda i: (i, 0)), pl.BlockSpec((1, d), lambda i: (0, 0))],
        out_specs=pl.BlockSpec((tm, d), lambda i: (i, 0)),
        out_shape=jax.ShapeDtypeStruct((s, d), out_dtype),
        compiler_params=_params("parallel"),
        name="rmsnorm",
    )(x, w.reshape(1, d))


def _mm_cast_res_kernel(a_ref, w_ref, r_ref, o_ref, wb_ref):
    @pl.when(pl.program_id(1) == 0)
    def _():
        wb_ref[...] = w_ref[...].astype(BF16)

    o_ref[...] = r_ref[...] + _dot(a_ref[...], wb_ref[...])


def _matmul_f32w_residual(a, w_stack, layer, res, name):
    m, k = a.shape
    n = w_stack.shape[2]
    tm = _pick(m, (1024, 512, 256, 128))
    tn = _pick(n, (512, 256, 128))
    return pl.pallas_call(
        _mm_cast_res_kernel,
        grid=(n // tn, m // tm),
        in_specs=[pl.BlockSpec((tm, k), lambda j, i: (i, 0)),
                  pl.BlockSpec((None, k, tn), lambda j, i: (layer, 0, j)),
                  pl.BlockSpec((tm, tn), lambda j, i: (i, j))],
        out_specs=pl.BlockSpec((tm, tn), lambda j, i: (i, j)),
        out_shape=jax.ShapeDtypeStruct((m, n), F32),
        scratch_shapes=[pltpu.VMEM((k, tn), BF16)],
        compiler_params=_params("parallel", "arbitrary"),
        name=name,
    )(a, w_stack, res)


def _mm_wt_kernel(a_ref, wt_ref, o_ref, wb_ref):
    @pl.when(pl.program_id(1) == 0)
    def _():
        wb_ref[...] = wt_ref[0].astype(BF16)

    o_ref[...] = _dot_nt(a_ref[...], wb_ref[...]).astype(o_ref.dtype)


def _matmul_wt(a, wt_stack, layer, segments, name):
    m, k = a.shape
    n = sum(n_run for _, n_run in segments)
    tm = _pick(m, (512, 256, 128))
    tn = 128
    for cand in (1024, 512, 256):
        if all(n_run % cand == 0 for _, n_run in segments):
            tn = cand
            break
    shifts, col = [], 0
    for row0, n_run in segments:
        shifts.append((col // tn, row0 - col))
        col += n_run

    def weight_row(j):
        row, applied = j * tn, 0
        for first_panel, shift in shifts:
            row = row + jnp.where(j >= first_panel, shift - applied, 0)
            applied = shift
        return pl.multiple_of(row, F32_SUBLANES)

    return pl.pallas_call(
        _mm_wt_kernel,
        grid=(n // tn, m // tm),
        in_specs=[pl.BlockSpec((tm, k), lambda j, i: (i, 0)),
                  pl.BlockSpec((pl.Element(1), pl.Element(tn), pl.Element(k)),
                               lambda j, i: (layer, weight_row(j), 0))],
        out_specs=pl.BlockSpec((tm, tn), lambda j, i: (i, j)),
        out_shape=jax.ShapeDtypeStruct((m, n), F32),
        scratch_shapes=[pltpu.VMEM((tn, k), BF16)],
        compiler_params=_params("parallel", "arbitrary", vmem=VMEM_LIMIT_WIDE_PANEL),
        name=name,
    )(a, wt_stack)


def _mm_res_kernel(a_ref, b_ref, r_ref, o_ref):
    o_ref[...] = r_ref[...] + _dot(a_ref[...], b_ref[...])


def _matmul_residual(a, b_stack, layer, res, name):
    m, k = a.shape
    n = b_stack.shape[2]
    tm = _pick(m, (1024, 512, 256, 128))
    tn = _pick(n, (256, 128))
    return pl.pallas_call(
        _mm_res_kernel,
        grid=(m // tm, n // tn),
        in_specs=[pl.BlockSpec((tm, k), lambda i, j: (i, 0), pipeline_mode=pl.Buffered(1)),
                  pl.BlockSpec((None, k, tn), lambda i, j: (layer, 0, j)),
                  pl.BlockSpec((tm, tn), lambda i, j: (i, j))],
        out_specs=pl.BlockSpec((tm, tn), lambda i, j: (i, j)),
        out_shape=jax.ShapeDtypeStruct((m, n), F32),
        compiler_params=_params("parallel", "arbitrary"),
        name=name,
    )(a, b_stack, res)


def _swiglu_kernel(a_ref, wg_ref, wu_ref, o_ref, wgb_ref, wub_ref):
    @pl.when(pl.program_id(1) == 0)
    def _():
        wgb_ref[...] = wg_ref[...].astype(BF16)
        wub_ref[...] = wu_ref[...].astype(BF16)

    a = a_ref[...]
    gate = _dot(a, wgb_ref[...])
    up = _dot(a, wub_ref[...])
    o_ref[...] = (gate * jax.nn.sigmoid(gate) * up).astype(o_ref.dtype)


def _ffn_in(h, w_stack, layer, d_ff):
    m, k = h.shape
    tm = _pick(m, (1024, 512, 256, 128))
    tn = _pick(d_ff, (256, 128))
    nb = d_ff // tn
    return pl.pallas_call(
        _swiglu_kernel,
        grid=(nb, m // tm),
        in_specs=[pl.BlockSpec((tm, k), lambda j, i: (i, 0)),
                  pl.BlockSpec((None, k, tn), lambda j, i: (layer, 0, j)),
                  pl.BlockSpec((None, k, tn), lambda j, i: (layer, 0, j + nb))],
        out_specs=pl.BlockSpec((tm, tn), lambda j, i: (i, j)),
        out_shape=jax.ShapeDtypeStruct((m, d_ff), BF16),
        scratch_shapes=[pltpu.VMEM((k, tn), BF16), pltpu.VMEM((k, tn), BF16)],
        compiler_params=_params("parallel", "arbitrary"),
        name="ffn_in",
    )(h, w_stack, w_stack)


def _gates_kernel(w_ref, h_ref, par_ref, o_ref, *, n_gdn, n_mlstm):
    z = _dot_nt(w_ref[...].astype(BF16), h_ref[...])
    r, tm = z.shape
    bias = par_ref[:, 0:1]
    scale = par_ref[:, 1:2]
    row = lax.broadcasted_iota(jnp.int32, (r, tm), 0)
    zb = z + bias
    soft = jnp.log1p(jnp.exp(-jnp.abs(zb)))
    softplus = jnp.maximum(zb, 0.0) + soft
    log_sig = jnp.minimum(zb, 0.0) - soft
    b0, b1, b2, b3 = 2 * n_gdn, 4 * n_gdn, 4 * n_gdn + 2 * n_mlstm, 4 * n_gdn + 4 * n_mlstm
    is_alpha = (row >= b0) & (row < b1)
    is_logf = (row >= b2) & (row < b3)
    raw = jnp.where(row < b0, jax.nn.sigmoid(zb),
                    jnp.where(is_alpha, scale * softplus,
                              jnp.where(row < b2, zb, jnp.where(is_logf, log_sig, 0.0))))
    pj = lax.broadcasted_iota(jnp.int32, (tm, tm), 0)
    pi = lax.broadcasted_iota(jnp.int32, (tm, tm), 1)
    same = (pj // CHUNK) == (pi // CHUNK)
    prefix = jnp.where(same & (pj <= pi), 1.0, 0.0)
    suffix = jnp.where(same & (pj >= pi), 1.0, 0.0)
    cum_p = _dot_f32(raw, prefix)
    cum_s = _dot_f32(raw, suffix)
    dir1 = (is_alpha & (row >= b0 + n_gdn)) | (is_logf & (row >= b2 + n_mlstm))
    cum = jnp.where(dir1, cum_s, cum_p)
    o_ref[...] = jnp.where(is_alpha | is_logf, cum, raw)


def _gates(h, w, par, n_gdn, n_mlstm):
    s, d = h.shape
    tm = _pick(s, (256, 128))
    return pl.pallas_call(
        functools.partial(_gates_kernel, n_gdn=n_gdn, n_mlstm=n_mlstm),
        grid=(s // tm,),
        in_specs=[pl.BlockSpec((GATE_ROWS, d), lambda i: (0, 0)),
                  pl.BlockSpec((tm, d), lambda i: (i, 0)),
                  pl.BlockSpec((GATE_ROWS, 2), lambda i: (0, 0))],
        out_specs=pl.BlockSpec((GATE_ROWS, tm), lambda i: (0, i)),
        out_shape=jax.ShapeDtypeStruct((GATE_ROWS, s), F32),
        compiler_params=_params("parallel"),
        name="gates",
    )(w, h, par)


def _rope_kernel(pos_ref, freq_ref, cos_ref, sin_ref):
    ang = pos_ref[...].astype(F32) * freq_ref[...]
    cos_ref[...] = jnp.cos(ang)
    sin_ref[...] = jnp.sin(ang)


def _rope_table(positions, half):
    s = positions.shape[0]
    tm = _pick(s, (512, 256, 128))
    inv_freq = (ROPE_BASE ** (-jnp.arange(half, dtype=F32) / half)).reshape(1, half)
    return pl.pallas_call(
        _rope_kernel,
        grid=(s // tm,),
        in_specs=[pl.BlockSpec((tm, 1), lambda i: (i, 0)), pl.BlockSpec((1, half), lambda i: (0, 0))],
        out_specs=[pl.BlockSpec((tm, half), lambda i: (i, 0))] * 2,
        out_shape=[jax.ShapeDtypeStruct((s, half), F32)] * 2,
        compiler_params=_params("parallel"),
        name="rope_table",
    )(positions.reshape(s, 1), inv_freq)


def _both_directions(make_spec, nc):
    return [make_spec(lambda c: c), make_spec(lambda c: nc - 1 - c)]


def _retention_kernel(lg_ref, qf_ref, kf_ref, vf_ref, cosf_ref, sinf_ref,
                      qb_ref, kb_ref, vb_ref, cosb_ref, sinb_ref, of_ref, ob_ref, state_ref, decay_ref,
                      *, heads):
    hg = pl.program_id(0)
    c = qf_ref.shape[0]
    dk = RET_HEAD_DIM
    half = dk // 2

    @pl.when(pl.program_id(1) == 0)
    def _():
        state_ref[...] = jnp.zeros_like(state_ref)
        i = lax.broadcasted_iota(jnp.int32, (c, c), 0)
        j = lax.broadcasted_iota(jnp.int32, (c, c), 1)
        dist = jnp.abs(i - j).astype(F32)
        for direction in range(2):
            mask = (j <= i) if direction == 0 else (j > i)
            for hh in range(heads):
                lg = jnp.full((1, 1), lg_ref[hg * heads + hh], F32)
                decay_ref[direction, hh] = jnp.where(mask, jnp.exp(lg * jnp.where(mask, dist, 0.0)), 0.0)

    idx = lax.broadcasted_iota(jnp.int32, (c, 1), 0).astype(F32)
    refs = ((qf_ref, kf_ref, vf_ref, cosf_ref, sinf_ref, of_ref),
            (qb_ref, kb_ref, vb_ref, cosb_ref, sinb_ref, ob_ref))
    chains = []
    for direction, (q_ref, k_ref, v_ref, cos_ref, sin_ref, o_ref) in enumerate(refs):
        cos = cos_ref[...]
        sin = sin_ref[...]

        def rot(t, cos=cos, sin=sin):
            t1, t2 = t[:, :half], t[:, half:]
            return jnp.concatenate([t1 * cos - t2 * sin, t1 * sin + t2 * cos], axis=1)

        for hh in range(heads):
            sl = slice(hh * dk, (hh + 1) * dk)
            lg = jnp.full((1, 1), lg_ref[hg * heads + hh], F32)
            decay_intra = decay_ref[direction, hh]
            q_decay = jnp.exp(lg * ((idx + 1.0) if direction == 0 else (c - idx)))
            k_decay = jnp.exp(lg * ((c - 1.0 - idx) if direction == 0 else idx))
            q = rot(q_ref[:, sl])
            k = rot(k_ref[:, sl]) * (dk ** -0.5)
            v = v_ref[:, sl].astype(BF16)
            scores = (_dot_nt(q.astype(BF16), k.astype(BF16)) * decay_intra).astype(BF16)
            chains.append((direction, hh, sl, o_ref, scores, v, (q * q_decay).astype(BF16),
                           (k * k_decay).astype(BF16), jnp.exp(lg * c)))
    states = [state_ref[direction, hh] for direction, hh, *_ in chains]
    for (direction, hh, sl, o_ref, scores, v, qd, kd, _), state in zip(chains, states):
        o_ref[:, sl] = _dot(scores, v) + _dot(qd, state.astype(BF16))
    for (direction, hh, sl, o_ref, scores, v, qd, kd, chunk_decay), state in zip(chains, states):
        state_ref[direction, hh] = state * chunk_decay + _dot_tn(kd, v)


def _retention(p, offs, tables, n_heads):
    s = p.shape[0]
    dk = RET_HEAD_DIM
    ck = _pick(s, (RET_CHUNK, CHUNK))
    nc = s // ck
    heads = _pick(n_heads, (4, 2, 1))
    wb = heads * dk
    cos, sin = tables
    qo, ko, vo = (o // wb for o in offs)
    log_gamma = jnp.log1p(-jnp.exp2(-5.0 - jnp.arange(n_heads, dtype=F32)))

    def specs(chunk):
        col = [pl.BlockSpec((ck, wb), lambda h, c, o=o: (chunk(c), o + h)) for o in (qo, ko, vo)]
        tab = pl.BlockSpec((ck, dk // 2), lambda h, c: (chunk(c), 0))
        return col + [tab, tab]

    fwd, bwd = _both_directions(specs, nc)
    out_f, out_b = _both_directions(lambda chunk: pl.BlockSpec((ck, wb), lambda h, c: (chunk(c), h)), nc)
    return pl.pallas_call(
        functools.partial(_retention_kernel, heads=heads),
        grid=(n_heads // heads, nc),
        in_specs=[pl.BlockSpec(memory_space=pltpu.SMEM)] + fwd + bwd,
        out_specs=[out_f, out_b],
        out_shape=[jax.ShapeDtypeStruct((s, n_heads * dk), F32)] * 2,
        scratch_shapes=[pltpu.VMEM((2, heads, dk, dk), F32), pltpu.VMEM((2, heads, ck, ck), F32)],
        compiler_params=_params("parallel", "arbitrary"),
        name="retention_scan",
    )(log_gamma, p, p, p, cos, sin, p, p, p, cos, sin)


def _conv_tile(prev_ref, cur_ref, next_ref, w_ref, o_ref, pad_ref, i, n_row_tiles, jb,
               *, n_qk_blocks, n_q_blocks, halo):
    tr, tc = cur_ref.shape
    keep_prev = jnp.where(i == 0, 0.0, 1.0)
    keep_next = jnp.where(i == n_row_tiles - 1, 0.0, 1.0)
    pad_ref[0:halo, :] = prev_ref[...] * keep_prev
    pad_ref[halo:halo + tr, :] = cur_ref[...]
    pad_ref[halo + tr:, :] = next_ref[...] * keep_next
    side = (CONV_WIDTH - 1) // 2
    padded = pad_ref[...]
    rows = tr + 2 * halo
    acc = jnp.zeros((tr, tc), F32)
    for tap in range(CONV_WIDTH):
        back = side - tap
        shifted = padded if back == 0 else pltpu.roll(padded, back % rows, 0)
        acc = acc + shifted[halo:halo + tr] * w_ref[tap:tap + 1, :]
    y = acc * jax.nn.sigmoid(acc)
    is_qk = jb < n_qk_blocks
    q_scale = jnp.where(jb < n_q_blocks, GDN_HEAD_DIM ** -0.5, 1.0)
    for g in range(tc // GDN_HEAD_DIM):
        sl = slice(g * GDN_HEAD_DIM, (g + 1) * GDN_HEAD_DIM)
        t = y[:, sl]
        normed = t * lax.rsqrt(jnp.sum(t * t, axis=-1, keepdims=True) + EPS) * q_scale
        o_ref[:, sl] = jnp.where(is_qk, normed, t)


def _conv_kernel(prev_ref, cur_ref, next_ref, w_ref, o_ref, pad_ref, **conv_kw):
    _conv_tile(prev_ref, cur_ref, next_ref, w_ref, o_ref, pad_ref,
               pl.program_id(0), pl.num_programs(0), pl.program_id(1), **conv_kw)


def _gdn_conv(p, off, width, conv_w):
    s = p.shape[0]
    halo = 8
    tr = _pick(s, (1024, 512, 256, 128))
    tc = _pick(width // 3, (512, 256, 128))
    ob = off // tc
    n_row8 = s // halo
    return pl.pallas_call(
        functools.partial(_conv_kernel, n_qk_blocks=2 * (width // 3) // tc, n_q_blocks=(width // 3) // tc,
                          halo=halo),
        grid=(s // tr, width // tc),
        in_specs=[pl.BlockSpec((halo, tc), lambda i, j: (jnp.maximum(i * (tr // halo) - 1, 0), ob + j)),
                  pl.BlockSpec((tr, tc), lambda i, j: (i, ob + j)),
                  pl.BlockSpec((halo, tc), lambda i, j: (jnp.minimum((i + 1) * (tr // halo), n_row8 - 1), ob + j)),
                  pl.BlockSpec((CONV_WIDTH, tc), lambda i, j: (0, j))],
        out_specs=pl.BlockSpec((tr, tc), lambda i, j: (i, j)),
        out_shape=jax.ShapeDtypeStruct((s, width), F32),
        scratch_shapes=[pltpu.VMEM((tr + 2 * halo, tc), F32)],
        compiler_params=_params("parallel", "parallel"),
        name="gdn_conv",
    )(p, p, p, conv_w)


def _unit_triangular_inverses(lms):
    c = lms[0].shape[0]
    i = lax.broadcasted_iota(jnp.int32, (c, c), 0)
    j = lax.broadcasted_iota(jnp.int32, (c, c), 1)
    level = jnp.bitwise_xor(i, j)
    eye = jnp.where(i == j, 1.0, 0.0)
    xs = [eye - jnp.where(level == 1, lm, 0.0) for lm in lms]
    span = 2
    while span < c:
        joins = (level >= span) & (level < 2 * span)
        ys = [_dot(jnp.where(joins, lm, 0.0).astype(BF16), x.astype(BF16)) for lm, x in zip(lms, xs)]
        xs = [x - _dot(x.astype(BF16), y.astype(BF16)) for x, y in zip(xs, ys)]
        span *= 2
    resids = [eye - x - _dot_split(lm, x) for lm, x in zip(lms, xs)]
    return [x + _dot(x.astype(BF16), r.astype(BF16)) for x, r in zip(xs, resids)]


def _gdn_prep_kernel(q_ref, k_ref, v_ref, beta_ref, cum_ref, u_ref, w_ref, attn_ref, qg_ref, kd_ref,
                     *, heads):
    c = q_ref.shape[0]
    dk = GDN_HEAD_DIM
    problems = [(direction, hh) for direction in range(2) for hh in range(heads)]
    masks = [_dir_masks(c, direction) for direction in range(2)]
    ks = [k_ref[:, hh * dk:(hh + 1) * dk] for hh in range(heads)]
    kbs = [k.astype(BF16) for k in ks]
    cum_rows = [cum_ref[direction, hh] for direction, hh in problems]
    cums = [_row_to_col(row) for row in cum_rows]
    betas = [_row_to_col(beta_ref[direction, hh]) for direction, hh in problems]
    decays = []
    for (direction, _), cum, row in zip(problems, cums, cum_rows):
        incl = masks[direction][0]
        decays.append(jnp.where(incl, jnp.exp(jnp.where(incl, cum - row, 0.0)), 0.0))
    k_betas = [ks[hh] * beta for (_, hh), beta in zip(problems, betas)]
    lowers = [_dot_nt(k_beta.astype(BF16), kbs[hh]) * jnp.where(masks[direction][1], decay, 0.0)
              for (direction, hh), k_beta, decay in zip(problems, k_betas, decays)]
    invs = _unit_triangular_inverses(lowers)
    e_cums = [jnp.exp(cum) for cum in cums]
    us = [_dot_split(inv, v_ref[:, hh * dk:(hh + 1) * dk] * beta)
          for (_, hh), inv, beta in zip(problems, invs, betas)]
    ws = [_dot_split(inv, k_beta * e_cum) for inv, k_beta, e_cum in zip(invs, k_betas, e_cums)]
    for n, (direction, hh) in enumerate(problems):
        sl = slice(hh * dk, (hh + 1) * dk)
        q = q_ref[:, sl]
        u_ref[direction, :, sl] = us[n]
        w_ref[direction, :, sl] = ws[n].astype(w_ref.dtype)
        attn_ref[direction, :, sl] = (_dot_nt(q.astype(BF16), kbs[hh]) * decays[n]).astype(attn_ref.dtype)
        qg_ref[direction, :, sl] = (q * e_cums[n]).astype(qg_ref.dtype)
        last = _lane_pick(cum_rows[n], c - 1 if direction == 0 else 0)
        kd_ref[direction, :, sl] = (ks[hh] * jnp.exp(last - cums[n])).astype(kd_ref.dtype)


def _gdn_scan_kernel(uf_ref, wf_ref, af_ref, qgf_ref, kdf_ref, cumf_ref,
                     ub_ref, wb_ref, ab_ref, qgb_ref, kdb_ref, cumb_ref, of_ref, ob_ref, state_ref,
                     *, heads):
    @pl.when(pl.program_id(1) == 0)
    def _():
        state_ref[...] = jnp.zeros_like(state_ref)

    c = uf_ref.shape[0]
    dk = GDN_HEAD_DIM
    refs = ((uf_ref, wf_ref, af_ref, qgf_ref, kdf_ref, cumf_ref, of_ref),
            (ub_ref, wb_ref, ab_ref, qgb_ref, kdb_ref, cumb_ref, ob_ref))
    chains = [(direction, hh, slice(hh * dk, (hh + 1) * dk)) for direction in range(2) for hh in range(heads)]
    states = [state_ref[direction, hh] for direction, hh, _ in chains]
    sbs = [state.astype(BF16) for state in states]
    v_news = [(refs[direction][0][:, sl] - _dot(refs[direction][1][:, sl], sb)).astype(BF16)
              for (direction, _, sl), sb in zip(chains, sbs)]
    for (direction, hh, sl), state, sb, v_new in zip(chains, states, sbs, v_news):
        _, _, attn_ref, qg_ref, kd_ref, cum_ref, o_ref = refs[direction]
        o_ref[:, sl] = _dot(attn_ref[:, sl], v_new) + _dot(qg_ref[:, sl], sb)
        last = _lane_pick(cum_ref[hh], c - 1 if direction == 0 else 0)
        state_ref[direction, hh] = state * jnp.exp(last) + _dot_tn(kd_ref[:, sl], v_new)


def _gdn(qkv, gates3, n_heads, beta_row0, cum_row0):
    s = qkv.shape[0]
    dk = GDN_HEAD_DIM
    nc = s // CHUNK
    width = n_heads * dk
    heads = _pick(n_heads, (8, 4, 2, 1))
    wb = heads * dk
    gate4 = gates3.reshape(GATE_ROWS // n_heads, n_heads // heads, heads, 1, s)

    def gate_rows(r0):
        return pl.BlockSpec((2, None, heads, 1, CHUNK), lambda h, c: (r0 // (2 * n_heads), h, 0, 0, c))

    cols = [pl.BlockSpec((CHUNK, wb), lambda h, c, o=o: (c, o * (n_heads // heads) + h)) for o in range(3)]
    out_blk = pl.BlockSpec((2, CHUNK, wb), lambda h, c: (0, c, h))
    u, w, attn, qg, kd = pl.pallas_call(
        functools.partial(_gdn_prep_kernel, heads=heads),
        grid=(n_heads // heads, nc),
        in_specs=cols + [gate_rows(beta_row0), gate_rows(cum_row0)],
        out_specs=[out_blk] * 5,
        out_shape=[jax.ShapeDtypeStruct((2, s, width), dt) for dt in (F32, BF16, BF16, BF16, BF16)],
        compiler_params=_params("parallel", "parallel"),
        name="gdn_prep",
    )(qkv, qkv, qkv, gate4, gate4)


    def specs(direction, chunk):
        seq = pl.BlockSpec((None, CHUNK, wb), lambda h, c: (direction, chunk(c), h))
        cum = pl.BlockSpec((None, None, heads, 1, CHUNK),
                           lambda h, c: (cum_row0 // n_heads + direction, h, 0, 0, chunk(c)))
        return [seq] * 5 + [cum]

    out_f, out_b = _both_directions(lambda chunk: pl.BlockSpec((CHUNK, wb), lambda h, c: (chunk(c), h)), nc)
    ops = (u, w, attn, qg, kd, gate4)
    return pl.pallas_call(
        functools.partial(_gdn_scan_kernel, heads=heads),
        grid=(n_heads // heads, nc),
        in_specs=specs(0, lambda c: c) + specs(1, lambda c: nc - 1 - c),
        out_specs=[out_f, out_b],
        out_shape=[jax.ShapeDtypeStruct((s, width), F32)] * 2,
        scratch_shapes=[pltpu.VMEM((2, heads, dk, dk), F32)],
        compiler_params=_params("parallel", "arbitrary"),
        name="gdn_scan",
    )(*ops, *ops)


def _mlstm_kernel(qf_ref, kf_ref, vf_ref, igf_ref, cumf_ref, qb_ref, kb_ref, vb_ref, igb_ref, cumb_ref,
                  of_ref, ob_ref, mem_ref, nrm_ref, m_ref, *, heads):
    @pl.when(pl.program_id(1) == 0)
    def _():
        mem_ref[...] = jnp.zeros_like(mem_ref)
        nrm_ref[...] = jnp.zeros_like(nrm_ref)
        m_ref[...] = jnp.zeros_like(m_ref)

    c = qf_ref.shape[0]
    dk = MLSTM_HEAD_DIM
    refs = ((qf_ref, kf_ref, vf_ref, igf_ref, cumf_ref, of_ref),
            (qb_ref, kb_ref, vb_ref, igb_ref, cumb_ref, ob_ref))
    chain_ids = [(direction, hh) for direction in range(2) for hh in range(heads)]

    def col(hh):
        return slice(hh * dk, (hh + 1) * dk)

    prepared = []
    for direction, hh in chain_ids:
        q_ref, k_ref, _, ig_ref, cum_ref, _ = refs[direction]
        incl, _ = _dir_masks(c, direction)
        ig_row = ig_ref[hh]
        cum_row = cum_ref[hh]
        ig = _row_to_col(ig_row)
        cum = _row_to_col(cum_row)
        log_d = jnp.where(incl, cum - cum_row + ig_row, -jnp.inf)
        d_max = jnp.max(log_d, axis=1, keepdims=True)
        qk = _dot_nt(q_ref[:, col(hh)].astype(BF16), (k_ref[:, col(hh)] * (dk ** -0.5)).astype(BF16))
        b_last = _lane_pick(cum_row, c - 1 if direction == 0 else 0)
        log_w = b_last - cum + ig
        w_max = jnp.max(log_w, axis=0, keepdims=True)
        prepared.append((cum, log_d, d_max, qk, b_last, log_w, w_max))
    m_prevs = [m_ref[direction, hh, 0:1, 0:1] for direction, hh in chain_ids]
    for (direction, hh), (cum, log_d, d_max, qk, *_), m_prev in zip(chain_ids, prepared, m_prevs):
        q_ref, _, v_ref, _, _, o_ref = refs[direction]
        q = q_ref[:, col(hh)]
        m_inter = cum + m_prev
        m_row = jnp.maximum(m_inter, d_max)
        s_mat = qk * jnp.exp(log_d - m_row)
        inter_scale = jnp.exp(m_inter - m_row)
        num = (_dot(s_mat.astype(BF16), v_ref[:, col(hh)].astype(BF16))
               + inter_scale * _dot(q.astype(BF16), mem_ref[direction, hh].astype(BF16)))
        den = (jnp.sum(s_mat, axis=1, keepdims=True)
               + inter_scale * jnp.sum(q * nrm_ref[direction, hh], axis=1, keepdims=True))
        o_ref[:, col(hh)] = num * (1.0 / jnp.maximum(jnp.abs(den), jnp.exp(-m_row)))
    for (direction, hh), (_, _, _, _, b_last, log_w, w_max), m_prev in zip(chain_ids, prepared, m_prevs):
        _, k_ref, v_ref, _, _, _ = refs[direction]
        m_new = jnp.maximum(b_last + m_prev, w_max)
        kw = k_ref[:, col(hh)] * ((dk ** -0.5) * jnp.exp(log_w - m_new))
        carry_scale = jnp.exp(b_last + m_prev - m_new)
        mem_ref[direction, hh] = (carry_scale * mem_ref[direction, hh]
                                  + _dot_tn(kw.astype(BF16), v_ref[:, col(hh)].astype(BF16)))
        nrm_ref[direction, hh] = carry_scale * nrm_ref[direction, hh] + jnp.sum(kw, axis=0, keepdims=True)
        m_ref[direction, hh] = jnp.broadcast_to(m_new, m_ref.shape[2:])


def _mlstm(p, offs, gates3, n_heads, ig_row0, cum_row0):
    s = p.shape[0]
    dk = MLSTM_HEAD_DIM
    nc = s // CHUNK
    heads = _pick(n_heads, (4, 2, 1))
    wb = heads * dk
    qo, ko, vo = (o // wb for o in offs)
    gate4 = gates3.reshape(GATE_ROWS // n_heads, n_heads // heads, heads, 1, s)

    def specs(direction, chunk):
        col = [pl.BlockSpec((CHUNK, wb), lambda h, c, o=o: (chunk(c), o + h)) for o in (qo, ko, vo)]
        rows = [pl.BlockSpec((None, None, heads, 1, CHUNK),
                             lambda h, c, r0=r0: (r0 // n_heads + direction, h, 0, 0, chunk(c)))
                for r0 in (ig_row0, cum_row0)]
        return col + rows

    out_f, out_b = _both_directions(lambda chunk: pl.BlockSpec((CHUNK, wb), lambda h, c: (chunk(c), h)), nc)
    return pl.pallas_call(
        functools.partial(_mlstm_kernel, heads=heads),
        grid=(n_heads // heads, nc),
        in_specs=specs(0, lambda c: c) + specs(1, lambda c: nc - 1 - c),
        out_specs=[out_f, out_b],
        out_shape=[jax.ShapeDtypeStruct((s, n_heads * dk), F32)] * 2,
        scratch_shapes=[pltpu.VMEM((2, heads, dk, dk), F32), pltpu.VMEM((2, heads, 1, dk), F32),
                        pltpu.VMEM((2, heads, 8, 128), F32)],
        compiler_params=_params("parallel", "arbitrary"),
        name="mlstm_scan",
    )(p, p, p, gate4, gate4, p, p, p, gate4, gate4)


def _finalize_kernel(yf_ref, yb_ref, g_ref, w_ref, o_ref, *, head_dim, subtract_mean, swish_gate):
    y = yf_ref[...] + yb_ref[...]
    g = g_ref[...]
    gate = jax.nn.sigmoid(g)
    if swish_gate:
        gate = g * gate
    w = w_ref[...]
    for hh in range(y.shape[1] // head_dim):
        sl = slice(hh * head_dim, (hh + 1) * head_dim)
        t = y[:, sl]
        if subtract_mean:
            t = t - jnp.mean(t, axis=-1, keepdims=True)
        t = t * lax.rsqrt(jnp.mean(t * t, axis=-1, keepdims=True) + EPS)
        o_ref[:, sl] = (t * w[:, sl] * gate[:, sl]).astype(o_ref.dtype)


def _finalize(y2, p, gate_off, norm_w, head_dim, subtract_mean, swish_gate):
    y_f, y_b = y2
    s, width = y_f.shape
    tm = _pick(s, (512, 256, 128))
    tc = _pick(width, (2048, 1024, 512, 256))
    go = gate_off // tc
    blk = pl.BlockSpec((tm, tc), lambda i, j: (i, j))
    return pl.pallas_call(
        functools.partial(_finalize_kernel, head_dim=head_dim, subtract_mean=subtract_mean,
                          swish_gate=swish_gate),
        grid=(s // tm, width // tc),
        in_specs=[blk, blk, pl.BlockSpec((tm, tc), lambda i, j: (i, go + j)),
                  pl.BlockSpec((1, tc), lambda i, j: (0, j))],
        out_specs=blk,
        out_shape=jax.ShapeDtypeStruct((s, width), BF16),
        compiler_params=_params("parallel", "parallel"),
        name="branch_finalize",
    )(y_f, y_b, p, norm_w.reshape(1, width))


def _merge_kernel(ya_ref, yb_ref, yc_ref, wb_ref, la_ref, lb_ref, lc_ref, o_ref):
    acc = jax.nn.sigmoid(la_ref[...]) * _dot(ya_ref[...], wb_ref[0])
    acc = acc + jax.nn.sigmoid(lb_ref[...]) * _dot(yb_ref[...], wb_ref[1])
    acc = acc + jax.nn.sigmoid(lc_ref[...]) * _dot(yc_ref[...], wb_ref[2])
    o_ref[...] = acc.astype(o_ref.dtype)


def _merge(ys, w_branch, layer, p, logit_off, d_model):
    s, width = ys[0].shape
    tm = _pick(s, (1024, 512, 256, 128))
    tn = _pick(d_model, (256, 128))
    nb = d_model // tn
    lo = logit_off // tn
    y_blk = pl.BlockSpec((tm, width), lambda i, j: (i, 0))

    def logit(r):
        return pl.BlockSpec((tm, tn), lambda i, j: (i, lo + r * nb + j))

    return pl.pallas_call(
        _merge_kernel,
        grid=(s // tm, nb),
        in_specs=[y_blk, y_blk, y_blk,
                  pl.BlockSpec((None, N_BRANCHES, width, tn), lambda i, j: (layer, 0, 0, j)),
                  logit(0), logit(1), logit(2)],
        out_specs=pl.BlockSpec((tm, tn), lambda i, j: (i, j)),
        out_shape=jax.ShapeDtypeStruct((s, d_model), BF16),
        compiler_params=_params("parallel", "parallel"),
        name="merge",
    )(*ys, w_branch, p, p, p)


def kernel(x, positions, norm_mix, w_in, conv_w, gdn_a_log, gdn_dt_bias, mlstm_i_bias, mlstm_f_bias,
           ret_norm, gdn_norm, mlstm_norm, w_branch, w_out, norm_ffn, w_ffn_in, w_ffn_out, norm_final):
    batch, s, d = x.shape
    assert batch == 1 and s % CHUNK == 0
    depth = w_in.shape[0]
    mix = d // 2
    ret_heads = mix // RET_HEAD_DIM
    gdn_heads = mix // GDN_HEAD_DIM
    mlstm_heads = mix // MLSTM_HEAD_DIM
    d_ff = w_ffn_out.shape[1]
    assert 4 * gdn_heads + 4 * mlstm_heads <= GATE_ROWS and GATE_ROWS % (2 * gdn_heads) == 0

    sizes = (mix, mix, mix, mix, 3 * mix, mix, 2 * gdn_heads, 2 * gdn_heads,
             mix, mix, mix, mix, 2 * mlstm_heads, 2 * mlstm_heads, N_BRANCHES * d)
    starts = np.concatenate([[0], np.cumsum(sizes)]).tolist()
    n_lead = starts[6]
    small = (6, 7, 12, 13)
    n_small = sum(sizes[idx] for idx in small)

    x2 = x.reshape(s, d)
    tables = _rope_table(positions.reshape(s), RET_HEAD_DIM // 2)
    w_in_t = jnp.swapaxes(w_in, 1, 2)
    w_branch_bf = w_branch.astype(BF16)
    w_ffn_out_bf = w_ffn_out.astype(BF16)

    for l in range(depth):
        w_small_t = jnp.concatenate([w_in_t[l, starts[idx]:starts[idx + 1], :] for idx in small], axis=0)
        w_small_t = jnp.pad(w_small_t, ((0, GATE_ROWS - n_small), (0, 0)))
        zeros = jnp.zeros((2 * gdn_heads,), F32)
        gate_bias = jnp.concatenate([zeros, gdn_dt_bias[l].reshape(-1), mlstm_i_bias[l].reshape(-1),
                                     mlstm_f_bias[l].reshape(-1)])
        gate_scale = jnp.concatenate([zeros + 1.0, -jnp.exp(gdn_a_log[l].reshape(-1)),
                                      jnp.ones((4 * mlstm_heads,), F32)])
        gate_par = jnp.pad(jnp.stack([gate_bias, gate_scale], axis=1), ((0, GATE_ROWS - n_small), (0, 0)))

        h = _rmsnorm(x2, norm_mix[l], BF16)
        p = _matmul_wt(h, w_in_t, l, ((0, n_lead), (starts[8], 4 * mix), (starts[14], N_BRANCHES * d)),
                       "in_proj")
        off_mlstm, off_logits = n_lead, n_lead + 4 * mix
        gates3 = _gates(h, w_small_t, gate_par, gdn_heads, mlstm_heads).reshape(GATE_ROWS, 1, s)

        y_ret = _retention(p, (starts[0], starts[1], starts[2]), tables, ret_heads)
        y_ret = _finalize(y_ret, p, starts[3], ret_norm[l], RET_HEAD_DIM, True, True)

        qkv = _gdn_conv(p, starts[4], 3 * mix, conv_w[l])
        y_gdn = _gdn(qkv, gates3, gdn_heads, 0, 2 * gdn_heads)
        y_gdn = _finalize(y_gdn, p, starts[5], gdn_norm[l], GDN_HEAD_DIM, False, True)

        y_ml = _mlstm(p, (off_mlstm, off_mlstm + mix, off_mlstm + 2 * mix), gates3, mlstm_heads,
                      4 * gdn_heads, 4 * gdn_heads + 2 * mlstm_heads)
        y_ml = _finalize(y_ml, p, off_mlstm + 3 * mix, mlstm_norm[l], MLSTM_HEAD_DIM, False, False)

        merged = _merge((y_ret, y_gdn, y_ml), w_branch_bf, l, p, off_logits, d)
        x2 = _matmul_f32w_residual(merged, w_out, l, x2, "out_proj")

        h = _rmsnorm(x2, norm_ffn[l], BF16)
        act = _ffn_in(h, w_ffn_in, l, d_ff)
        x2 = _matmul_residual(act, w_ffn_out_bf, l, x2, "ffn_out")

    return _rmsnorm(x2, norm_final, x.dtype).reshape(batch, s, d)
```

```python
import functools

import numpy as np
import jax
import jax.numpy as jnp
from jax import lax
from jax.experimental import pallas as pl
from jax.experimental.pallas import tpu as pltpu

RET_HEAD_DIM = 256
GDN_HEAD_DIM = 128
MLSTM_HEAD_DIM = 256
CONV_WIDTH = 5
ROPE_BASE = 10000.0
N_BRANCHES = 3
EPS = 1e-6

CHUNK = 128
RET_CHUNK = 256
GATE_ROWS = 128
F32_SUBLANES = 8
BF16_SUBLANES = 16
V7X_VMEM_BYTES = 64 * 1024 * 1024
VMEM_LIMIT = V7X_VMEM_BYTES - 8 * 1024 * 1024
VMEM_LIMIT_WIDE_PANEL = V7X_VMEM_BYTES - 4 * 1024 * 1024

F32 = jnp.float32
BF16 = jnp.bfloat16


def _params(*sem, vmem=VMEM_LIMIT):
    return pltpu.CompilerParams(dimension_semantics=sem, vmem_limit_bytes=vmem)


def _pick(n, candidates):
    for c in candidates:
        if n % c == 0:
            return c
    raise ValueError(f"no tile in {candidates} divides {n}")


def _dot(a, b):
    return jnp.dot(a, b, preferred_element_type=F32)


def _dot_nt(a, b):
    return lax.dot_general(a, b, (((1,), (1,)), ((), ())), preferred_element_type=F32)


def _dot_tn(a, b):
    return lax.dot_general(a, b, (((0,), (0,)), ((), ())), preferred_element_type=F32)


def _dot_f32(a, b):
    return jnp.dot(a, b, preferred_element_type=F32, precision=lax.Precision.HIGHEST)


def _split(a):
    hi = a.astype(BF16)
    return hi, (a - hi.astype(F32)).astype(BF16)


def _dot_split(a, b):
    a_hi, a_lo = _split(a)
    b_hi, b_lo = _split(b)
    return _dot(a_hi, b_hi) + (_dot(a_hi, b_lo) + _dot(a_lo, b_hi))


def _row_to_col(row):
    c = row.shape[-1]
    i = lax.broadcasted_iota(jnp.int32, (c, c), 0)
    j = lax.broadcasted_iota(jnp.int32, (c, c), 1)
    return jnp.sum(jnp.where(i == j, jnp.broadcast_to(row, (c, c)), 0.0), axis=1, keepdims=True)


def _lane_pick(row, idx):
    lane = lax.broadcasted_iota(jnp.int32, row.shape, 1)
    return jnp.sum(jnp.where(lane == idx, row, 0.0), axis=1, keepdims=True)


def _dir_masks(c, direction):
    i = lax.broadcasted_iota(jnp.int32, (c, c), 0)
    j = lax.broadcasted_iota(jnp.int32, (c, c), 1)
    if direction == 0:
        return j <= i, j < i
    return j >= i, j > i


def _rmsnorm_kernel(x_ref, w_ref, o_ref):
    x = x_ref[...]
    y = x * lax.rsqrt(jnp.mean(x * x, axis=-1, keepdims=True) + EPS)
    o_ref[...] = (y * w_ref[...]).astype(o_ref.dtype)


def _rmsnorm(x, w, out_dtype):
    s, d = x.shape
    tm = _pick(s, (512, 256, 128))
    return pl.pallas_call(
        _rmsnorm_kernel,
        grid=(s // tm,),
        in_specs=[pl.BlockSpec((tm, d), lambda i: (i, 0)), pl.BlockSpec((1, d), lambda i: (0, 0))],
        out_specs=pl.BlockSpec((tm, d), lambda i: (i, 0)),
        out_shape=jax.ShapeDtypeStruct((s, d), out_dtype),
        compiler_params=_params("parallel"),
        name="rmsnorm",
    )(x, w.reshape(1, d))


def _mm_cast_res_kernel(a_ref, w_ref, r_ref, o_ref, wb_ref):
    @pl.when(pl.program_id(1) == 0)
    def _():
        wb_ref[...] = w_ref[...].astype(BF16)

    o_ref[...] = r_ref[...] + _dot(a_ref[...], wb_ref[...])


def _matmul_f32w_residual(a, w_stack, layer, res, name):
    m, k = a.shape
    n = w_stack.shape[2]
    tm = _pick(m, (1024, 512, 256, 128))
    tn = _pick(n, (512, 256, 128))
    return pl.pallas_call(
        _mm_cast_res_kernel,
        grid=(n // tn, m // tm),
        in_specs=[pl.BlockSpec((tm, k), lambda j, i: (i, 0)),
                  pl.BlockSpec((None, k, tn), lambda j, i: (layer, 0, j)),
                  pl.BlockSpec((tm, tn), lambda j, i: (i, j))],
        out_specs=pl.BlockSpec((tm, tn), lambda j, i: (i, j)),
        out_shape=jax.ShapeDtypeStruct((m, n), F32),
        scratch_shapes=[pltpu.VMEM((k, tn), BF16)],
        compiler_params=_params("parallel", "arbitrary"),
        name=name,
    )(a, w_stack, res)


def _mm_wt_kernel(a_ref, wt_ref, o_ref, wb_ref):
    @pl.when(pl.program_id(1) == 0)
    def _():
        wb_ref[...] = wt_ref[0].astype(BF16)

    o_ref[...] = _dot_nt(a_ref[...], wb_ref[...]).astype(o_ref.dtype)


def _matmul_wt(a, wt_stack, layer, segments, name):
    m, k = a.shape
    n = sum(n_run for _, n_run in segments)
    tm = _pick(m, (512, 256, 128))
    tn = 128
    for cand in (1024, 512, 256):
        if all(n_run % cand == 0 for _, n_run in segments):
            tn = cand
            break
    shifts, col = [], 0
    for row0, n_run in segments:
        shifts.append((col // tn, row0 - col))
        col += n_run

    def weight_row(j):
        row, applied = j * tn, 0
        for first_panel, shift in shifts:
            row = row + jnp.where(j >= first_panel, shift - applied, 0)
            applied = shift
        return pl.multiple_of(row, F32_SUBLANES)

    return pl.pallas_call(
        _mm_wt_kernel,
        grid=(n // tn, m // tm),
        in_specs=[pl.BlockSpec((tm, k), lambda j, i: (i, 0)),
                  pl.BlockSpec((pl.Element(1), pl.Element(tn), pl.Element(k)),
                               lambda j, i: (layer, weight_row(j), 0))],
        out_specs=pl.BlockSpec((tm, tn), lambda j, i: (i, j)),
        out_shape=jax.ShapeDtypeStruct((m, n), BF16),
        scratch_shapes=[pltpu.VMEM((tn, k), BF16)],
        compiler_params=_params("parallel", "arbitrary", vmem=VMEM_LIMIT_WIDE_PANEL),
        name=name,
    )(a, wt_stack)


def _mm_res_kernel(a_ref, b_ref, r_ref, o_ref):
    o_ref[...] = r_ref[...] + _dot(a_ref[...], b_ref[...])


def _matmul_residual(a, b_stack, layer, res, name):
    m, k = a.shape
    n = b_stack.shape[2]
    tm = _pick(m, (1024, 512, 256, 128))
    tn = _pick(n, (256, 128))
    return pl.pallas_call(
        _mm_res_kernel,
        grid=(m // tm, n // tn),
        in_specs=[pl.BlockSpec((tm, k), lambda i, j: (i, 0), pipeline_mode=pl.Buffered(1)),
                  pl.BlockSpec((None, k, tn), lambda i, j: (layer, 0, j)),
                  pl.BlockSpec((tm, tn), lambda i, j: (i, j))],
        out_specs=pl.BlockSpec((tm, tn), lambda i, j: (i, j)),
        out_shape=jax.ShapeDtypeStruct((m, n), F32),
        compiler_params=_params("parallel", "arbitrary"),
        name=name,
    )(a, b_stack, res)


def _swiglu_kernel(a_ref, wg_ref, wu_ref, o_ref, wgb_ref, wub_ref):
    @pl.when(pl.program_id(1) == 0)
    def _():
        wgb_ref[...] = wg_ref[...].astype(BF16)
        wub_ref[...] = wu_ref[...].astype(BF16)

    a = a_ref[...]
    gate = _dot(a, wgb_ref[...])
    up = _dot(a, wub_ref[...])
    o_ref[...] = (gate * jax.nn.sigmoid(gate) * up).astype(o_ref.dtype)


def _ffn_in(h, w_stack, layer, d_ff):
    m, k = h.shape
    tm = _pick(m, (1024, 512, 256, 128))
    tn = _pick(d_ff, (256, 128))
    nb = d_ff // tn
    return pl.pallas_call(
        _swiglu_kernel,
        grid=(nb, m // tm),
        in_specs=[pl.BlockSpec((tm, k), lambda j, i: (i, 0)),
                  pl.BlockSpec((None, k, tn), lambda j, i: (layer, 0, j)),
                  pl.BlockSpec((None, k, tn), lambda j, i: (layer, 0, j + nb))],
        out_specs=pl.BlockSpec((tm, tn), lambda j, i: (i, j)),
        out_shape=jax.ShapeDtypeStruct((m, d_ff), BF16),
        scratch_shapes=[pltpu.VMEM((k, tn), BF16), pltpu.VMEM((k, tn), BF16)],
        compiler_params=_params("parallel", "arbitrary"),
        name="ffn_in",
    )(h, w_stack, w_stack)


def _gates_kernel(w_ref, h_ref, par_ref, o_ref, *, n_gdn, n_mlstm):
    z = _dot_nt(w_ref[...].astype(BF16), h_ref[...])
    r, tm = z.shape
    bias = par_ref[:, 0:1]
    scale = par_ref[:, 1:2]
    row = lax.broadcasted_iota(jnp.int32, (r, tm), 0)
    zb = z + bias
    soft = jnp.log1p(jnp.exp(-jnp.abs(zb)))
    softplus = jnp.maximum(zb, 0.0) + soft
    log_sig = jnp.minimum(zb, 0.0) - soft
    b0, b1, b2, b3 = 2 * n_gdn, 4 * n_gdn, 4 * n_gdn + 2 * n_mlstm, 4 * n_gdn + 4 * n_mlstm
    is_alpha = (row >= b0) & (row < b1)
    is_logf = (row >= b2) & (row < b3)
    raw = jnp.where(row < b0, jax.nn.sigmoid(zb),
                    jnp.where(is_alpha, scale * softplus,
                              jnp.where(row < b2, zb, jnp.where(is_logf, log_sig, 0.0))))
    pj = lax.broadcasted_iota(jnp.int32, (tm, tm), 0)
    pi = lax.broadcasted_iota(jnp.int32, (tm, tm), 1)
    same = (pj // CHUNK) == (pi // CHUNK)
    prefix = jnp.where(same & (pj <= pi), 1.0, 0.0)
    suffix = jnp.where(same & (pj >= pi), 1.0, 0.0)
    cum_p = _dot_f32(raw, prefix)
    cum_s = _dot_f32(raw, suffix)
    dir1 = (is_alpha & (row >= b0 + n_gdn)) | (is_logf & (row >= b2 + n_mlstm))
    cum = jnp.where(dir1, cum_s, cum_p)
    o_ref[...] = jnp.where(is_alpha | is_logf, cum, raw)


def _gates(h, w, par, n_gdn, n_mlstm):
    s, d = h.shape
    tm = _pick(s, (256, 128))
    return pl.pallas_call(
        functools.partial(_gates_kernel, n_gdn=n_gdn, n_mlstm=n_mlstm),
        grid=(s // tm,),
        in_specs=[pl.BlockSpec((GATE_ROWS, d), lambda i: (0, 0)),
                  pl.BlockSpec((tm, d), lambda i: (i, 0)),
                  pl.BlockSpec((GATE_ROWS, 2), lambda i: (0, 0))],
        out_specs=pl.BlockSpec((GATE_ROWS, tm), lambda i: (0, i)),
        out_shape=jax.ShapeDtypeStruct((GATE_ROWS, s), F32),
        compiler_params=_params("parallel"),
        name="gates",
    )(w, h, par)


def _rope_kernel(pos_ref, freq_ref, cos_ref, sin_ref):
    ang = pos_ref[...].astype(F32) * freq_ref[...]
    cos_ref[...] = jnp.cos(ang)
    sin_ref[...] = jnp.sin(ang)


def _rope_table(positions, half):
    s = positions.shape[0]
    tm = _pick(s, (512, 256, 128))
    inv_freq = (ROPE_BASE ** (-jnp.arange(half, dtype=F32) / half)).reshape(1, half)
    return pl.pallas_call(
        _rope_kernel,
        grid=(s // tm,),
        in_specs=[pl.BlockSpec((tm, 1), lambda i: (i, 0)), pl.BlockSpec((1, half), lambda i: (0, 0))],
        out_specs=[pl.BlockSpec((tm, half), lambda i: (i, 0))] * 2,
        out_shape=[jax.ShapeDtypeStruct((s, half), F32)] * 2,
        compiler_params=_params("parallel"),
        name="rope_table",
    )(positions.reshape(s, 1), inv_freq)


def _both_directions(make_spec, nc):
    return [make_spec(lambda c: c), make_spec(lambda c: nc - 1 - c)]


def _retention_kernel(lg_ref, qf_ref, kf_ref, vf_ref, cosf_ref, sinf_ref,
                      qb_ref, kb_ref, vb_ref, cosb_ref, sinb_ref, of_ref, ob_ref, state_ref, decay_ref,
                      *, heads):
    hg = pl.program_id(0)
    c = qf_ref.shape[0]
    dk = RET_HEAD_DIM
    half = dk // 2

    @pl.when(pl.program_id(1) == 0)
    def _():
        state_ref[...] = jnp.zeros_like(state_ref)
        i = lax.broadcasted_iota(jnp.int32, (c, c), 0)
        j = lax.broadcasted_iota(jnp.int32, (c, c), 1)
        dist = jnp.abs(i - j).astype(F32)
        for direction in range(2):
            mask = (j <= i) if direction == 0 else (j > i)
            for hh in range(heads):
                lg = jnp.full((1, 1), lg_ref[hg * heads + hh], F32)
                decay_ref[direction, hh] = jnp.where(mask, jnp.exp(lg * jnp.where(mask, dist, 0.0)), 0.0)

    idx = lax.broadcasted_iota(jnp.int32, (c, 1), 0).astype(F32)
    refs = ((qf_ref, kf_ref, vf_ref, cosf_ref, sinf_ref, of_ref),
            (qb_ref, kb_ref, vb_ref, cosb_ref, sinb_ref, ob_ref))
    chains = []
    for direction, (q_ref, k_ref, v_ref, cos_ref, sin_ref, o_ref) in enumerate(refs):
        cos = cos_ref[...]
        sin = sin_ref[...]

        def rot(t, cos=cos, sin=sin):
            t1, t2 = t[:, :half], t[:, half:]
            return jnp.concatenate([t1 * cos - t2 * sin, t1 * sin + t2 * cos], axis=1)

        for hh in range(heads):
            sl = slice(hh * dk, (hh + 1) * dk)
            lg = jnp.full((1, 1), lg_ref[hg * heads + hh], F32)
            decay_intra = decay_ref[direction, hh]
            q_decay = jnp.exp(lg * ((idx + 1.0) if direction == 0 else (c - idx)))
            k_decay = jnp.exp(lg * ((c - 1.0 - idx) if direction == 0 else idx))
            q = rot(q_ref[:, sl].astype(F32))
            k = rot(k_ref[:, sl].astype(F32)) * (dk ** -0.5)
            v = v_ref[:, sl]
            scores = (_dot_nt(q.astype(BF16), k.astype(BF16)) * decay_intra).astype(BF16)
            chains.append((direction, hh, sl, o_ref, scores, v, (q * q_decay).astype(BF16),
                           (k * k_decay).astype(BF16), jnp.exp(lg * c)))
    states = [state_ref[direction, hh] for direction, hh, *_ in chains]
    for (direction, hh, sl, o_ref, scores, v, qd, kd, _), state in zip(chains, states):
        o_ref[:, sl] = _dot(scores, v) + _dot(qd, state.astype(BF16))
    for (direction, hh, sl, o_ref, scores, v, qd, kd, chunk_decay), state in zip(chains, states):
        state_ref[direction, hh] = state * chunk_decay + _dot_tn(kd, v)


def _retention(p, offs, tables, n_heads):
    s = p.shape[0]
    dk = RET_HEAD_DIM
    ck = _pick(s, (RET_CHUNK, CHUNK))
    nc = s // ck
    heads = _pick(n_heads, (4, 2, 1))
    wb = heads * dk
    cos, sin = tables
    qo, ko, vo = (o // wb for o in offs)
    log_gamma = jnp.log1p(-jnp.exp2(-5.0 - jnp.arange(n_heads, dtype=F32)))

    def specs(chunk):
        col = [pl.BlockSpec((ck, wb), lambda h, c, o=o: (chunk(c), o + h)) for o in (qo, ko, vo)]
        tab = pl.BlockSpec((ck, dk // 2), lambda h, c: (chunk(c), 0))
        return col + [tab, tab]

    fwd, bwd = _both_directions(specs, nc)
    out_f, out_b = _both_directions(lambda chunk: pl.BlockSpec((ck, wb), lambda h, c: (chunk(c), h)), nc)
    return pl.pallas_call(
        functools.partial(_retention_kernel, heads=heads),
        grid=(n_heads // heads, nc),
        in_specs=[pl.BlockSpec(memory_space=pltpu.SMEM)] + fwd + bwd,
        out_specs=[out_f, out_b],
        out_shape=[jax.ShapeDtypeStruct((s, n_heads * dk), F32)] * 2,
        scratch_shapes=[pltpu.VMEM((2, heads, dk, dk), F32), pltpu.VMEM((2, heads, ck, ck), F32)],
        compiler_params=_params("parallel", "arbitrary"),
        name="retention_scan",
    )(log_gamma, p, p, p, cos, sin, p, p, p, cos, sin)


def _conv_tile(prev_ref, cur_ref, next_ref, w_ref, o_ref, pad_ref, i, n_row_tiles, jb,
               *, n_qk_blocks, n_q_blocks, halo):
    tr, tc = cur_ref.shape
    keep_prev = jnp.where(i == 0, 0.0, 1.0)
    keep_next = jnp.where(i == n_row_tiles - 1, 0.0, 1.0)
    pad_ref[0:halo, :] = prev_ref[...].astype(F32) * keep_prev
    pad_ref[halo:halo + tr, :] = cur_ref[...].astype(F32)
    pad_ref[halo + tr:, :] = next_ref[...].astype(F32) * keep_next
    side = (CONV_WIDTH - 1) // 2
    padded = pad_ref[...]
    rows = tr + 2 * halo
    acc = jnp.zeros((tr, tc), F32)
    for tap in range(CONV_WIDTH):
        back = side - tap
        shifted = padded if back == 0 else pltpu.roll(padded, back % rows, 0)
        acc = acc + shifted[halo:halo + tr] * w_ref[tap:tap + 1, :]
    y = acc * jax.nn.sigmoid(acc)
    is_qk = jb < n_qk_blocks
    q_scale = jnp.where(jb < n_q_blocks, GDN_HEAD_DIM ** -0.5, 1.0)
    for g in range(tc // GDN_HEAD_DIM):
        sl = slice(g * GDN_HEAD_DIM, (g + 1) * GDN_HEAD_DIM)
        t = y[:, sl]
        normed = t * lax.rsqrt(jnp.sum(t * t, axis=-1, keepdims=True) + EPS) * q_scale
        o_ref[:, sl] = jnp.where(is_qk, normed, t)


def _conv_kernel(prev_ref, cur_ref, next_ref, w_ref, o_ref, pad_ref, **conv_kw):
    _conv_tile(prev_ref, cur_ref, next_ref, w_ref, o_ref, pad_ref,
               pl.program_id(0), pl.num_programs(0), pl.program_id(1), **conv_kw)


def _gdn_conv(p, off, width, conv_w):
    s = p.shape[0]
    halo = BF16_SUBLANES
    tr = _pick(s, (1024, 512, 256, 128))
    tc = _pick(width // 3, (512, 256, 128))
    ob = off // tc
    n_halo_blocks = s // halo
    return pl.pallas_call(
        functools.partial(_conv_kernel, n_qk_blocks=2 * (width // 3) // tc, n_q_blocks=(width // 3) // tc,
                          halo=halo),
        grid=(s // tr, width // tc),
        in_specs=[pl.BlockSpec((halo, tc), lambda i, j: (jnp.maximum(i * (tr // halo) - 1, 0), ob + j)),
                  pl.BlockSpec((tr, tc), lambda i, j: (i, ob + j)),
                  pl.BlockSpec((halo, tc), lambda i, j: (jnp.minimum((i + 1) * (tr // halo), n_halo_blocks - 1), ob + j)),
                  pl.BlockSpec((CONV_WIDTH, tc), lambda i, j: (0, j))],
        out_specs=pl.BlockSpec((tr, tc), lambda i, j: (i, j)),
        out_shape=jax.ShapeDtypeStruct((s, width), F32),
        scratch_shapes=[pltpu.VMEM((tr + 2 * halo, tc), F32)],
        compiler_params=_params("parallel", "parallel"),
        name="gdn_conv",
    )(p, p, p, conv_w)


def _unit_triangular_inverses(lms):
    c = lms[0].shape[0]
    i = lax.broadcasted_iota(jnp.int32, (c, c), 0)
    j = lax.broadcasted_iota(jnp.int32, (c, c), 1)
    level = jnp.bitwise_xor(i, j)
    eye = jnp.where(i == j, 1.0, 0.0)
    xs = [eye - jnp.where(level == 1, lm, 0.0) for lm in lms]
    span = 2
    while span < c:
        joins = (level >= span) & (level < 2 * span)
        ys = [_dot(jnp.where(joins, lm, 0.0).astype(BF16), x.astype(BF16)) for lm, x in zip(lms, xs)]
        xs = [x - _dot(x.astype(BF16), y.astype(BF16)) for x, y in zip(xs, ys)]
        span *= 2
    resids = [eye - x - _dot_split(lm, x) for lm, x in zip(lms, xs)]
    return [x + _dot(x.astype(BF16), r.astype(BF16)) for x, r in zip(xs, resids)]


def _gdn_prep_kernel(q_ref, k_ref, v_ref, beta_ref, cum_ref, u_ref, w_ref, attn_ref, qg_ref, kd_ref,
                     *, heads):
    c = q_ref.shape[0]
    dk = GDN_HEAD_DIM
    problems = [(direction, hh) for direction in range(2) for hh in range(heads)]
    masks = [_dir_masks(c, direction) for direction in range(2)]
    ks = [k_ref[:, hh * dk:(hh + 1) * dk] for hh in range(heads)]
    kbs = [k.astype(BF16) for k in ks]
    cum_rows = [cum_ref[direction, hh] for direction, hh in problems]
    cums = [_row_to_col(row) for row in cum_rows]
    betas = [_row_to_col(beta_ref[direction, hh]) for direction, hh in problems]
    decays = []
    for (direction, _), cum, row in zip(problems, cums, cum_rows):
        incl = masks[direction][0]
        decays.append(jnp.where(incl, jnp.exp(jnp.where(incl, cum - row, 0.0)), 0.0))
    k_betas = [ks[hh] * beta for (_, hh), beta in zip(problems, betas)]
    lowers = [_dot_nt(k_beta.astype(BF16), kbs[hh]) * jnp.where(masks[direction][1], decay, 0.0)
              for (direction, hh), k_beta, decay in zip(problems, k_betas, decays)]
    invs = _unit_triangular_inverses(lowers)
    e_cums = [jnp.exp(cum) for cum in cums]
    us = [_dot_split(inv, v_ref[:, hh * dk:(hh + 1) * dk] * beta)
          for (_, hh), inv, beta in zip(problems, invs, betas)]
    ws = [_dot_split(inv, k_beta * e_cum) for inv, k_beta, e_cum in zip(invs, k_betas, e_cums)]
    for n, (direction, hh) in enumerate(problems):
        sl = slice(hh * dk, (hh + 1) * dk)
        q = q_ref[:, sl]
        u_ref[direction, :, sl] = us[n]
        w_ref[direction, :, sl] = ws[n].astype(w_ref.dtype)
        attn_ref[direction, :, sl] = (_dot_nt(q.astype(BF16), kbs[hh]) * decays[n]).astype(attn_ref.dtype)
        qg_ref[direction, :, sl] = (q * e_cums[n]).astype(qg_ref.dtype)
        last = _lane_pick(cum_rows[n], c - 1 if direction == 0 else 0)
        kd_ref[direction, :, sl] = (ks[hh] * jnp.exp(last - cums[n])).astype(kd_ref.dtype)


def _gdn_scan_kernel(uf_ref, wf_ref, af_ref, qgf_ref, kdf_ref, cumf_ref,
                     ub_ref, wb_ref, ab_ref, qgb_ref, kdb_ref, cumb_ref, of_ref, ob_ref, state_ref,
                     *, heads):
    @pl.when(pl.program_id(1) == 0)
    def _():
        state_ref[...] = jnp.zeros_like(state_ref)

    c = uf_ref.shape[0]
    dk = GDN_HEAD_DIM
    refs = ((uf_ref, wf_ref, af_ref, qgf_ref, kdf_ref, cumf_ref, of_ref),
            (ub_ref, wb_ref, ab_ref, qgb_ref, kdb_ref, cumb_ref, ob_ref))
    chains = [(direction, hh, slice(hh * dk, (hh + 1) * dk)) for direction in range(2) for hh in range(heads)]
    states = [state_ref[direction, hh] for direction, hh, _ in chains]
    sbs = [state.astype(BF16) for state in states]
    v_news = [(refs[direction][0][:, sl] - _dot(refs[direction][1][:, sl], sb)).astype(BF16)
              for (direction, _, sl), sb in zip(chains, sbs)]
    for (direction, hh, sl), state, sb, v_new in zip(chains, states, sbs, v_news):
        _, _, attn_ref, qg_ref, kd_ref, cum_ref, o_ref = refs[direction]
        o_ref[:, sl] = _dot(attn_ref[:, sl], v_new) + _dot(qg_ref[:, sl], sb)
        last = _lane_pick(cum_ref[hh], c - 1 if direction == 0 else 0)
        state_ref[direction, hh] = state * jnp.exp(last) + _dot_tn(kd_ref[:, sl], v_new)


def _gdn(qkv, gates3, n_heads, beta_row0, cum_row0):
    s = qkv.shape[0]
    dk = GDN_HEAD_DIM
    nc = s // CHUNK
    width = n_heads * dk
    heads = _pick(n_heads, (8, 4, 2, 1))
    wb = heads * dk
    gate4 = gates3.reshape(GATE_ROWS // n_heads, n_heads // heads, heads, 1, s)

    def gate_rows(r0):
        return pl.BlockSpec((2, None, heads, 1, CHUNK), lambda h, c: (r0 // (2 * n_heads), h, 0, 0, c))

    cols = [pl.BlockSpec((CHUNK, wb), lambda h, c, o=o: (c, o * (n_heads // heads) + h)) for o in range(3)]
    out_blk = pl.BlockSpec((2, CHUNK, wb), lambda h, c: (0, c, h))
    u, w, attn, qg, kd = pl.pallas_call(
        functools.partial(_gdn_prep_kernel, heads=heads),
        grid=(n_heads // heads, nc),
        in_specs=cols + [gate_rows(beta_row0), gate_rows(cum_row0)],
        out_specs=[out_blk] * 5,
        out_shape=[jax.ShapeDtypeStruct((2, s, width), dt) for dt in (F32, BF16, BF16, BF16, BF16)],
        compiler_params=_params("parallel", "parallel"),
        name="gdn_prep",
    )(qkv, qkv, qkv, gate4, gate4)


    def specs(direction, chunk):
        seq = pl.BlockSpec((None, CHUNK, wb), lambda h, c: (direction, chunk(c), h))
        cum = pl.BlockSpec((None, None, heads, 1, CHUNK),
                           lambda h, c: (cum_row0 // n_heads + direction, h, 0, 0, chunk(c)))
        return [seq] * 5 + [cum]

    out_f, out_b = _both_directions(lambda chunk: pl.BlockSpec((CHUNK, wb), lambda h, c: (chunk(c), h)), nc)
    ops = (u, w, attn, qg, kd, gate4)
    return pl.pallas_call(
        functools.partial(_gdn_scan_kernel, heads=heads),
        grid=(n_heads // heads, nc),
        in_specs=specs(0, lambda c: c) + specs(1, lambda c: nc - 1 - c),
        out_specs=[out_f, out_b],
        out_shape=[jax.ShapeDtypeStruct((s, width), F32)] * 2,
        scratch_shapes=[pltpu.VMEM((2, heads, dk, dk), F32)],
        compiler_params=_params("parallel", "arbitrary"),
        name="gdn_scan",
    )(*ops, *ops)


def _mlstm_kernel(qf_ref, kf_ref, vf_ref, igf_ref, cumf_ref, qb_ref, kb_ref, vb_ref, igb_ref, cumb_ref,
                  of_ref, ob_ref, mem_ref, nrm_ref, m_ref, *, heads):
    @pl.when(pl.program_id(1) == 0)
    def _():
        mem_ref[...] = jnp.zeros_like(mem_ref)
        nrm_ref[...] = jnp.zeros_like(nrm_ref)
        m_ref[...] = jnp.zeros_like(m_ref)

    c = qf_ref.shape[0]
    dk = MLSTM_HEAD_DIM
    refs = ((qf_ref, kf_ref, vf_ref, igf_ref, cumf_ref, of_ref),
            (qb_ref, kb_ref, vb_ref, igb_ref, cumb_ref, ob_ref))
    chain_ids = [(direction, hh) for direction in range(2) for hh in range(heads)]

    def col(hh):
        return slice(hh * dk, (hh + 1) * dk)

    prepared = []
    for direction, hh in chain_ids:
        q_ref, k_ref, _, ig_ref, cum_ref, _ = refs[direction]
        incl, _ = _dir_masks(c, direction)
        ig_row = ig_ref[hh]
        cum_row = cum_ref[hh]
        ig = _row_to_col(ig_row)
        cum = _row_to_col(cum_row)
        log_d = jnp.where(incl, cum - cum_row + ig_row, -jnp.inf)
        d_max = jnp.max(log_d, axis=1, keepdims=True)
        qk = _dot_nt(q_ref[:, col(hh)].astype(BF16), (k_ref[:, col(hh)] * (dk ** -0.5)).astype(BF16))
        b_last = _lane_pick(cum_row, c - 1 if direction == 0 else 0)
        log_w = b_last - cum + ig
        w_max = jnp.max(log_w, axis=0, keepdims=True)
        prepared.append((cum, log_d, d_max, qk, b_last, log_w, w_max))
    m_prevs = [m_ref[direction, hh, 0:1, 0:1] for direction, hh in chain_ids]
    for (direction, hh), (cum, log_d, d_max, qk, *_), m_prev in zip(chain_ids, prepared, m_prevs):
        q_ref, _, v_ref, _, _, o_ref = refs[direction]
        q = q_ref[:, col(hh)]
        m_inter = cum + m_prev
        m_row = jnp.maximum(m_inter, d_max)
        s_mat = qk * jnp.exp(log_d - m_row)
        inter_scale = jnp.exp(m_inter - m_row)
        num = (_dot(s_mat.astype(BF16), v_ref[:, col(hh)].astype(BF16))
               + inter_scale * _dot(q.astype(BF16), mem_ref[direction, hh].astype(BF16)))
        den = (jnp.sum(s_mat, axis=1, keepdims=True)
               + inter_scale * jnp.sum(q * nrm_ref[direction, hh], axis=1, keepdims=True))
        o_ref[:, col(hh)] = num * (1.0 / jnp.maximum(jnp.abs(den), jnp.exp(-m_row)))
    for (direction, hh), (_, _, _, _, b_last, log_w, w_max), m_prev in zip(chain_ids, prepared, m_prevs):
        _, k_ref, v_ref, _, _, _ = refs[direction]
        m_new = jnp.maximum(b_last + m_prev, w_max)
        kw = k_ref[:, col(hh)] * ((dk ** -0.5) * jnp.exp(log_w - m_new))
        carry_scale = jnp.exp(b_last + m_prev - m_new)
        mem_ref[direction, hh] = (carry_scale * mem_ref[direction, hh]
                                  + _dot_tn(kw.astype(BF16), v_ref[:, col(hh)].astype(BF16)))
        nrm_ref[direction, hh] = carry_scale * nrm_ref[direction, hh] + jnp.sum(kw, axis=0, keepdims=True)
        m_ref[direction, hh] = jnp.broadcast_to(m_new, m_ref.shape[2:])


def _mlstm(p, offs, gates3, n_heads, ig_row0, cum_row0):
    s = p.shape[0]
    dk = MLSTM_HEAD_DIM
    nc = s // CHUNK
    heads = _pick(n_heads, (4, 2, 1))
    wb = heads * dk
    qo, ko, vo = (o // wb for o in offs)
    gate4 = gates3.reshape(GATE_ROWS // n_heads, n_heads // heads, heads, 1, s)

    def specs(direction, chunk):
        col = [pl.BlockSpec((CHUNK, wb), lambda h, c, o=o: (chunk(c), o + h)) for o in (qo, ko, vo)]
        rows = [pl.BlockSpec((None, None, heads, 1, CHUNK),
                             lambda h, c, r0=r0: (r0 // n_heads + direction, h, 0, 0, chunk(c)))
                for r0 in (ig_row0, cum_row0)]
        return col + rows

    out_f, out_b = _both_directions(lambda chunk: pl.BlockSpec((CHUNK, wb), lambda h, c: (chunk(c), h)), nc)
    return pl.pallas_call(
        functools.partial(_mlstm_kernel, heads=heads),
        grid=(n_heads // heads, nc),
        in_specs=specs(0, lambda c: c) + specs(1, lambda c: nc - 1 - c),
        out_specs=[out_f, out_b],
        out_shape=[jax.ShapeDtypeStruct((s, n_heads * dk), F32)] * 2,
        scratch_shapes=[pltpu.VMEM((2, heads, dk, dk), F32), pltpu.VMEM((2, heads, 1, dk), F32),
                        pltpu.VMEM((2, heads, 8, 128), F32)],
        compiler_params=_params("parallel", "arbitrary"),
        name="mlstm_scan",
    )(p, p, p, gate4, gate4, p, p, p, gate4, gate4)


def _finalize_kernel(yf_ref, yb_ref, g_ref, w_ref, o_ref, *, head_dim, subtract_mean, swish_gate):
    y = yf_ref[...] + yb_ref[...]
    g = g_ref[...].astype(F32)
    gate = jax.nn.sigmoid(g)
    if swish_gate:
        gate = g * gate
    w = w_ref[...]
    for hh in range(y.shape[1] // head_dim):
        sl = slice(hh * head_dim, (hh + 1) * head_dim)
        t = y[:, sl]
        if subtract_mean:
            t = t - jnp.mean(t, axis=-1, keepdims=True)
        t = t * lax.rsqrt(jnp.mean(t * t, axis=-1, keepdims=True) + EPS)
        o_ref[:, sl] = (t * w[:, sl] * gate[:, sl]).astype(o_ref.dtype)


def _finalize(y2, p, gate_off, norm_w, head_dim, subtract_mean, swish_gate):
    y_f, y_b = y2
    s, width = y_f.shape
    tm = _pick(s, (512, 256, 128))
    tc = _pick(width, (2048, 1024, 512, 256))
    go = gate_off // tc
    blk = pl.BlockSpec((tm, tc), lambda i, j: (i, j))
    return pl.pallas_call(
        functools.partial(_finalize_kernel, head_dim=head_dim, subtract_mean=subtract_mean,
                          swish_gate=swish_gate),
        grid=(s // tm, width // tc),
        in_specs=[blk, blk, pl.BlockSpec((tm, tc), lambda i, j: (i, go + j)),
                  pl.BlockSpec((1, tc), lambda i, j: (0, j))],
        out_specs=blk,
        out_shape=jax.ShapeDtypeStruct((s, width), BF16),
        compiler_params=_params("parallel", "parallel"),
        name="branch_finalize",
    )(y_f, y_b, p, norm_w.reshape(1, width))


def _merge_kernel(ya_ref, yb_ref, yc_ref, wb_ref, la_ref, lb_ref, lc_ref, o_ref):
    acc = jax.nn.sigmoid(la_ref[...].astype(F32)) * _dot(ya_ref[...], wb_ref[0])
    acc = acc + jax.nn.sigmoid(lb_ref[...].astype(F32)) * _dot(yb_ref[...], wb_ref[1])
    acc = acc + jax.nn.sigmoid(lc_ref[...].astype(F32)) * _dot(yc_ref[...], wb_ref[2])
    o_ref[...] = acc.astype(o_ref.dtype)


def _merge(ys, w_branch, layer, p, logit_off, d_model):
    s, width = ys[0].shape
    tm = _pick(s, (1024, 512, 256, 128))
    tn = _pick(d_model, (256, 128))
    nb = d_model // tn
    lo = logit_off // tn
    y_blk = pl.BlockSpec((tm, width), lambda i, j: (i, 0))

    def logit(r):
        return pl.BlockSpec((tm, tn), lambda i, j: (i, lo + r * nb + j))

    return pl.pallas_call(
        _merge_kernel,
        grid=(s // tm, nb),
        in_specs=[y_blk, y_blk, y_blk,
                  pl.BlockSpec((None, N_BRANCHES, width, tn), lambda i, j: (layer, 0, 0, j)),
                  logit(0), logit(1), logit(2)],
        out_specs=pl.BlockSpec((tm, tn), lambda i, j: (i, j)),
        out_shape=jax.ShapeDtypeStruct((s, d_model), BF16),
        compiler_params=_params("parallel", "parallel"),
        name="merge",
    )(*ys, w_branch, p, p, p)


def kernel(x, positions, norm_mix, w_in, conv_w, gdn_a_log, gdn_dt_bias, mlstm_i_bias, mlstm_f_bias,
           ret_norm, gdn_norm, mlstm_norm, w_branch, w_out, norm_ffn, w_ffn_in, w_ffn_out, norm_final):
    batch, s, d = x.shape
    assert batch == 1 and s % CHUNK == 0
    depth = w_in.shape[0]
    mix = d // 2
    ret_heads = mix // RET_HEAD_DIM
    gdn_heads = mix // GDN_HEAD_DIM
    mlstm_heads = mix // MLSTM_HEAD_DIM
    d_ff = w_ffn_out.shape[1]
    assert 4 * gdn_heads + 4 * mlstm_heads <= GATE_ROWS and GATE_ROWS % (2 * gdn_heads) == 0

    sizes = (mix, mix, mix, mix, 3 * mix, mix, 2 * gdn_heads, 2 * gdn_heads,
             mix, mix, mix, mix, 2 * mlstm_heads, 2 * mlstm_heads, N_BRANCHES * d)
    starts = np.concatenate([[0], np.cumsum(sizes)]).tolist()
    n_lead = starts[6]
    small = (6, 7, 12, 13)
    n_small = sum(sizes[idx] for idx in small)

    x2 = x.reshape(s, d)
    tables = _rope_table(positions.reshape(s), RET_HEAD_DIM // 2)
    w_in_t = jnp.swapaxes(w_in, 1, 2)
    w_branch_bf = w_branch.astype(BF16)
    w_ffn_out_bf = w_ffn_out.astype(BF16)

    for l in range(depth):
        w_small_t = jnp.concatenate([w_in_t[l, starts[idx]:starts[idx + 1], :] for idx in small], axis=0)
        w_small_t = jnp.pad(w_small_t, ((0, GATE_ROWS - n_small), (0, 0)))
        zeros = jnp.zeros((2 * gdn_heads,), F32)
        gate_bias = jnp.concatenate([zeros, gdn_dt_bias[l].reshape(-1), mlstm_i_bias[l].reshape(-1),
                                     mlstm_f_bias[l].reshape(-1)])
        gate_scale = jnp.concatenate([zeros + 1.0, -jnp.exp(gdn_a_log[l].reshape(-1)),
                                      jnp.ones((4 * mlstm_heads,), F32)])
        gate_par = jnp.pad(jnp.stack([gate_bias, gate_scale], axis=1), ((0, GATE_ROWS - n_small), (0, 0)))

        h = _rmsnorm(x2, norm_mix[l], BF16)
        p = _matmul_wt(h, w_in_t, l, ((0, n_lead), (starts[8], 4 * mix), (starts[14], N_BRANCHES * d)),
                       "in_proj")
        off_mlstm, off_logits = n_lead, n_lead + 4 * mix
        gates3 = _gates(h, w_small_t, gate_par, gdn_heads, mlstm_heads).reshape(GATE_ROWS, 1, s)

        y_ret = _retention(p, (starts[0], starts[1], starts[2]), tables, ret_heads)
        y_ret = _finalize(y_ret, p, starts[3], ret_norm[l], RET_HEAD_DIM, True, True)

        qkv = _gdn_conv(p, starts[4], 3 * mix, conv_w[l])
        y_gdn = _gdn(qkv, gates3, gdn_heads, 0, 2 * gdn_heads)
        y_gdn = _finalize(y_gdn, p, starts[5], gdn_norm[l], GDN_HEAD_DIM, False, True)

        y_ml = _mlstm(p, (off_mlstm, off_mlstm + mix, off_mlstm + 2 * mix), gates3, mlstm_heads,
                      4 * gdn_heads, 4 * gdn_heads + 2 * mlstm_heads)
        y_ml = _finalize(y_ml, p, off_mlstm + 3 * mix, mlstm_norm[l], MLSTM_HEAD_DIM, False, False)

        merged = _merge((y_ret, y_gdn, y_ml), w_branch_bf, l, p, off_logits, d)
        x2 = _matmul_f32w_residual(merged, w_out, l, x2, "out_proj")

        h = _rmsnorm(x2, norm_ffn[l], BF16)
        act = _ffn_in(h, w_ffn_in, l, d_ff)
        x2 = _matmul_residual(act, w_ffn_out_bf, l, x2, "ffn_out")

    return _rmsnorm(x2, norm_final, x.dtype).reshape(batch, s, d)
```

```python
import functools

import numpy as np
import jax
import jax.numpy as jnp
from jax import lax
from jax.experimental import pallas as pl
from jax.experimental.pallas import tpu as pltpu

RET_HEAD_DIM = 256
GDN_HEAD_DIM = 128
MLSTM_HEAD_DIM = 256
CONV_WIDTH = 5
ROPE_BASE = 10000.0
N_BRANCHES = 3
EPS = 1e-6

CHUNK = 128
RET_CHUNK = 256
GATE_ROWS = 128
F32_SUBLANES = 8
BF16_SUBLANES = 16
V7X_VMEM_BYTES = 64 * 1024 * 1024
VMEM_LIMIT = V7X_VMEM_BYTES - 8 * 1024 * 1024
VMEM_LIMIT_WIDE_PANEL = V7X_VMEM_BYTES - 4 * 1024 * 1024

F32 = jnp.float32
BF16 = jnp.bfloat16


def _params(*sem, vmem=VMEM_LIMIT):
    return pltpu.CompilerParams(dimension_semantics=sem, vmem_limit_bytes=vmem)


def _pick(n, candidates):
    for c in candidates:
        if n % c == 0:
            return c
    raise ValueError(f"no tile in {candidates} divides {n}")


def _dot(a, b):
    return jnp.dot(a, b, preferred_element_type=F32)


def _dot_nt(a, b):
    return lax.dot_general(a, b, (((1,), (1,)), ((), ())), preferred_element_type=F32)


def _dot_tn(a, b):
    return lax.dot_general(a, b, (((0,), (0,)), ((), ())), preferred_element_type=F32)


def _dot_f32(a, b):
    return jnp.dot(a, b, preferred_element_type=F32, precision=lax.Precision.HIGHEST)


def _split(a):
    hi = a.astype(BF16)
    return hi, (a - hi.astype(F32)).astype(BF16)


def _dot_split(a, b):
    a_hi, a_lo = _split(a)
    b_hi, b_lo = _split(b)
    return _dot(a_hi, b_hi) + (_dot(a_hi, b_lo) + _dot(a_lo, b_hi))


def _row_to_col(row):
    c = row.shape[-1]
    i = lax.broadcasted_iota(jnp.int32, (c, c), 0)
    j = lax.broadcasted_iota(jnp.int32, (c, c), 1)
    return jnp.sum(jnp.where(i == j, jnp.broadcast_to(row, (c, c)), 0.0), axis=1, keepdims=True)


def _lane_pick(row, idx):
    lane = lax.broadcasted_iota(jnp.int32, row.shape, 1)
    return jnp.sum(jnp.where(lane == idx, row, 0.0), axis=1, keepdims=True)


def _dir_masks(c, direction):
    i = lax.broadcasted_iota(jnp.int32, (c, c), 0)
    j = lax.broadcasted_iota(jnp.int32, (c, c), 1)
    if direction == 0:
        return j <= i, j < i
    return j >= i, j > i


def _rmsnorm_kernel(x_ref, w_ref, o_ref):
    x = x_ref[...]
    y = x * lax.rsqrt(jnp.mean(x * x, axis=-1, keepdims=True) + EPS)
    o_ref[...] = (y * w_ref[...]).astype(o_ref.dtype)


def _rmsnorm(x, w, out_dtype):
    s, d = x.shape
    tm = _pick(s, (512, 256, 128))
    return pl.pallas_call(
        _rmsnorm_kernel,
        grid=(s // tm,),
        in_specs=[pl.BlockSpec((tm, d), lambda i: (i, 0)), pl.BlockSpec((1, d), lambda i: (0, 0))],
        out_specs=pl.BlockSpec((tm, d), lambda i: (i, 0)),
        out_shape=jax.ShapeDtypeStruct((s, d), out_dtype),
        compiler_params=_params("parallel"),
        name="rmsnorm",
    )(x, w.reshape(1, d))


def _mm_cast_res_kernel(a_ref, w_ref, r_ref, o_ref, wb_ref):
    @pl.when(pl.program_id(1) == 0)
    def _():
        wb_ref[...] = w_ref[...].astype(BF16)

    o_ref[...] = r_ref[...] + _dot(a_ref[...], wb_ref[...])


def _matmul_f32w_residual(a, w_stack, layer, res, name):
    m, k = a.shape
    n = w_stack.shape[2]
    tm = _pick(m, (1024, 512, 256, 128))
    tn = _pick(n, (512, 256, 128))
    return pl.pallas_call(
        _mm_cast_res_kernel,
        grid=(n // tn, m // tm),
        in_specs=[pl.BlockSpec((tm, k), lambda j, i: (i, 0)),
                  pl.BlockSpec((None, k, tn), lambda j, i: (layer, 0, j)),
                  pl.BlockSpec((tm, tn), lambda j, i: (i, j))],
        out_specs=pl.BlockSpec((tm, tn), lambda j, i: (i, j)),
        out_shape=jax.ShapeDtypeStruct((m, n), F32),
        scratch_shapes=[pltpu.VMEM((k, tn), BF16)],
        compiler_params=_params("parallel", "arbitrary"),
        name=name,
    )(a, w_stack, res)


def _mm_wt_kernel(a_ref, wt_ref, o_ref, wb_ref):
    @pl.when(pl.program_id(1) == 0)
    def _():
        wb_ref[...] = wt_ref[0].astype(BF16)

    o_ref[...] = _dot_nt(a_ref[...], wb_ref[...]).astype(o_ref.dtype)


def _matmul_wt(a, wt_stack, layer, segments, name):
    m, k = a.shape
    n = sum(n_run for _, n_run in segments)
    tm = _pick(m, (512, 256, 128))
    tn = 128
    for cand in (1024, 512, 256):
        if all(n_run % cand == 0 for _, n_run in segments):
            tn = cand
            break
    shifts, col = [], 0
    for row0, n_run in segments:
        shifts.append((col // tn, row0 - col))
        col += n_run

    def weight_row(j):
        row, applied = j * tn, 0
        for first_panel, shift in shifts:
            row = row + jnp.where(j >= first_panel, shift - applied, 0)
            applied = shift
        return pl.multiple_of(row, F32_SUBLANES)

    return pl.pallas_call(
        _mm_wt_kernel,
        grid=(n // tn, m // tm),
        in_specs=[pl.BlockSpec((tm, k), lambda j, i: (i, 0)),
                  pl.BlockSpec((pl.Element(1), pl.Element(tn), pl.Element(k)),
                               lambda j, i: (layer, weight_row(j), 0))],
        out_specs=pl.BlockSpec((tm, tn), lambda j, i: (i, j)),
        out_shape=jax.ShapeDtypeStruct((m, n), BF16),
        scratch_shapes=[pltpu.VMEM((tn, k), BF16)],
        compiler_params=_params("parallel", "arbitrary", vmem=VMEM_LIMIT_WIDE_PANEL),
        name=name,
    )(a, wt_stack)


def _mm_res_kernel(a_ref, b_ref, r_ref, o_ref):
    o_ref[...] = r_ref[...] + _dot(a_ref[...], b_ref[...])


def _matmul_residual(a, b_stack, layer, res, name):
    m, k = a.shape
    n = b_stack.shape[2]
    tm = _pick(m, (1024, 512, 256, 128))
    tn = _pick(n, (256, 128))
    return pl.pallas_call(
        _mm_res_kernel,
        grid=(m // tm, n // tn),
        in_specs=[pl.BlockSpec((tm, k), lambda i, j: (i, 0), pipeline_mode=pl.Buffered(1)),
                  pl.BlockSpec((None, k, tn), lambda i, j: (layer, 0, j)),
                  pl.BlockSpec((tm, tn), lambda i, j: (i, j))],
        out_specs=pl.BlockSpec((tm, tn), lambda i, j: (i, j)),
        out_shape=jax.ShapeDtypeStruct((m, n), F32),
        compiler_params=_params("parallel", "arbitrary"),
        name=name,
    )(a, b_stack, res)


def _swiglu_kernel(a_ref, wg_ref, wu_ref, o_ref, wgb_ref, wub_ref):
    @pl.when(pl.program_id(1) == 0)
    def _():
        wgb_ref[...] = wg_ref[...].astype(BF16)
        wub_ref[...] = wu_ref[...].astype(BF16)

    a = a_ref[...]
    gate = _dot(a, wgb_ref[...])
    up = _dot(a, wub_ref[...])
    o_ref[...] = (gate * jax.nn.sigmoid(gate) * up).astype(o_ref.dtype)


def _ffn_in(h, w_stack, layer, d_ff):
    m, k = h.shape
    tm = _pick(m, (1024, 512, 256, 128))
    tn = _pick(d_ff, (256, 128))
    nb = d_ff // tn
    return pl.pallas_call(
        _swiglu_kernel,
        grid=(nb, m // tm),
        in_specs=[pl.BlockSpec((tm, k), lambda j, i: (i, 0)),
                  pl.BlockSpec((None, k, tn), lambda j, i: (layer, 0, j)),
                  pl.BlockSpec((None, k, tn), lambda j, i: (layer, 0, j + nb))],
        out_specs=pl.BlockSpec((tm, tn), lambda j, i: (i, j)),
        out_shape=jax.ShapeDtypeStruct((m, d_ff), BF16),
        scratch_shapes=[pltpu.VMEM((k, tn), BF16), pltpu.VMEM((k, tn), BF16)],
        compiler_params=_params("parallel", "arbitrary"),
        name="ffn_in",
    )(h, w_stack, w_stack)


def _gates_kernel(w_ref, h_ref, par_ref, o_ref, *, n_gdn, n_mlstm):
    z = _dot_nt(w_ref[...].astype(BF16), h_ref[...])
    r, tm = z.shape
    bias = par_ref[:, 0:1]
    scale = par_ref[:, 1:2]
    row = lax.broadcasted_iota(jnp.int32, (r, tm), 0)
    zb = z + bias
    soft = jnp.log1p(jnp.exp(-jnp.abs(zb)))
    softplus = jnp.maximum(zb, 0.0) + soft
    log_sig = jnp.minimum(zb, 0.0) - soft
    b0, b1, b2, b3 = 2 * n_gdn, 4 * n_gdn, 4 * n_gdn + 2 * n_mlstm, 4 * n_gdn + 4 * n_mlstm
    is_alpha = (row >= b0) & (row < b1)
    is_logf = (row >= b2) & (row < b3)
    raw = jnp.where(row < b0, jax.nn.sigmoid(zb),
                    jnp.where(is_alpha, scale * softplus,
                              jnp.where(row < b2, zb, jnp.where(is_logf, log_sig, 0.0))))
    pj = lax.broadcasted_iota(jnp.int32, (tm, tm), 0)
    pi = lax.broadcasted_iota(jnp.int32, (tm, tm), 1)
    same = (pj // CHUNK) == (pi // CHUNK)
    prefix = jnp.where(same & (pj <= pi), 1.0, 0.0)
    suffix = jnp.where(same & (pj >= pi), 1.0, 0.0)
    cum_p = _dot_f32(raw, prefix)
    cum_s = _dot_f32(raw, suffix)
    dir1 = (is_alpha & (row >= b0 + n_gdn)) | (is_logf & (row >= b2 + n_mlstm))
    cum = jnp.where(dir1, cum_s, cum_p)
    o_ref[...] = jnp.where(is_alpha | is_logf, cum, raw)


def _gates(h, w, par, n_gdn, n_mlstm):
    s, d = h.shape
    tm = _pick(s, (256, 128))
    return pl.pallas_call(
        functools.partial(_gates_kernel, n_gdn=n_gdn, n_mlstm=n_mlstm),
        grid=(s // tm,),
        in_specs=[pl.BlockSpec((GATE_ROWS, d), lambda i: (0, 0)),
                  pl.BlockSpec((tm, d), lambda i: (i, 0)),
                  pl.BlockSpec((GATE_ROWS, 2), lambda i: (0, 0))],
        out_specs=pl.BlockSpec((GATE_ROWS, tm), lambda i: (0, i)),
        out_shape=jax.ShapeDtypeStruct((GATE_ROWS, s), F32),
        compiler_params=_params("parallel"),
        name="gates",
    )(w, h, par)


def _rope_kernel(pos_ref, freq_ref, cos_ref, sin_ref):
    ang = pos_ref[...].astype(F32) * freq_ref[...]
    cos_ref[...] = jnp.cos(ang)
    sin_ref[...] = jnp.sin(ang)


def _rope_table(positions, half):
    s = positions.shape[0]
    tm = _pick(s, (512, 256, 128))
    inv_freq = (ROPE_BASE ** (-jnp.arange(half, dtype=F32) / half)).reshape(1, half)
    return pl.pallas_call(
        _rope_kernel,
        grid=(s // tm,),
        in_specs=[pl.BlockSpec((tm, 1), lambda i: (i, 0)), pl.BlockSpec((1, half), lambda i: (0, 0))],
        out_specs=[pl.BlockSpec((tm, half), lambda i: (i, 0))] * 2,
        out_shape=[jax.ShapeDtypeStruct((s, half), F32)] * 2,
        compiler_params=_params("parallel"),
        name="rope_table",
    )(positions.reshape(s, 1), inv_freq)


def _both_directions(make_spec, nc):
    return [make_spec(lambda c: c), make_spec(lambda c: nc - 1 - c)]


def _retention_kernel(lg_ref, qf_ref, kf_ref, vf_ref, cosf_ref, sinf_ref,
                      qb_ref, kb_ref, vb_ref, cosb_ref, sinb_ref, of_ref, ob_ref, state_ref, decay_ref,
                      *, heads):
    hg = pl.program_id(0)
    c = qf_ref.shape[0]
    dk = RET_HEAD_DIM
    half = dk // 2

    @pl.when(pl.program_id(1) == 0)
    def _():
        state_ref[...] = jnp.zeros_like(state_ref)
        i = lax.broadcasted_iota(jnp.int32, (c, c), 0)
        j = lax.broadcasted_iota(jnp.int32, (c, c), 1)
        dist = jnp.abs(i - j).astype(F32)
        for direction in range(2):
            mask = (j <= i) if direction == 0 else (j > i)
            for hh in range(heads):
                lg = jnp.full((1, 1), lg_ref[hg * heads + hh], F32)
                decay_ref[direction, hh] = jnp.where(mask, jnp.exp(lg * jnp.where(mask, dist, 0.0)), 0.0)

    idx = lax.broadcasted_iota(jnp.int32, (c, 1), 0).astype(F32)
    refs = ((qf_ref, kf_ref, vf_ref, cosf_ref, sinf_ref, of_ref),
            (qb_ref, kb_ref, vb_ref, cosb_ref, sinb_ref, ob_ref))
    chains = []
    for direction, (q_ref, k_ref, v_ref, cos_ref, sin_ref, o_ref) in enumerate(refs):
        cos = cos_ref[...]
        sin = sin_ref[...]

        def rot(t, cos=cos, sin=sin):
            t1, t2 = t[:, :half], t[:, half:]
            return jnp.concatenate([t1 * cos - t2 * sin, t1 * sin + t2 * cos], axis=1)

        for hh in range(heads):
            sl = slice(hh * dk, (hh + 1) * dk)
            lg = jnp.full((1, 1), lg_ref[hg * heads + hh], F32)
            decay_intra = decay_ref[direction, hh]
            q_decay = jnp.exp(lg * ((idx + 1.0) if direction == 0 else (c - idx)))
            k_decay = jnp.exp(lg * ((c - 1.0 - idx) if direction == 0 else idx))
            q = rot(q_ref[:, sl].astype(F32))
            k = rot(k_ref[:, sl].astype(F32)) * (dk ** -0.5)
            v = v_ref[:, sl]
            scores = (_dot_nt(q.astype(BF16), k.astype(BF16)) * decay_intra).astype(BF16)
            chains.append((direction, hh, sl, o_ref, scores, v, (q * q_decay).astype(BF16),
                           (k * k_decay).astype(BF16), jnp.exp(lg * c)))
    states = [state_ref[direction, hh] for direction, hh, *_ in chains]
    for (direction, hh, sl, o_ref, scores, v, qd, kd, _), state in zip(chains, states):
        o_ref[:, sl] = _dot(scores, v) + _dot(qd, state.astype(BF16))
    for (direction, hh, sl, o_ref, scores, v, qd, kd, chunk_decay), state in zip(chains, states):
        state_ref[direction, hh] = state * chunk_decay + _dot_tn(kd, v)


def _retention(p, offs, tables, n_heads):
    s = p.shape[0]
    dk = RET_HEAD_DIM
    ck = _pick(s, (RET_CHUNK, CHUNK))
    nc = s // ck
    heads = _pick(n_heads, (4, 2, 1))
    wb = heads * dk
    cos, sin = tables
    qo, ko, vo = (o // wb for o in offs)
    log_gamma = jnp.log1p(-jnp.exp2(-5.0 - jnp.arange(n_heads, dtype=F32)))

    def specs(chunk):
        col = [pl.BlockSpec((ck, wb), lambda h, c, o=o: (chunk(c), o + h)) for o in (qo, ko, vo)]
        tab = pl.BlockSpec((ck, dk // 2), lambda h, c: (chunk(c), 0))
        return col + [tab, tab]

    fwd, bwd = _both_directions(specs, nc)
    out_f, out_b = _both_directions(lambda chunk: pl.BlockSpec((ck, wb), lambda h, c: (chunk(c), h)), nc)
    return pl.pallas_call(
        functools.partial(_retention_kernel, heads=heads),
        grid=(n_heads // heads, nc),
        in_specs=[pl.BlockSpec(memory_space=pltpu.SMEM)] + fwd + bwd,
        out_specs=[out_f, out_b],
        out_shape=[jax.ShapeDtypeStruct((s, n_heads * dk), F32)] * 2,
        scratch_shapes=[pltpu.VMEM((2, heads, dk, dk), F32), pltpu.VMEM((2, heads, ck, ck), F32)],
        compiler_params=_params("parallel", "arbitrary"),
        name="retention_scan",
    )(log_gamma, p, p, p, cos, sin, p, p, p, cos, sin)


def _conv_tile(prev_ref, cur_ref, next_ref, w_ref, o_ref, pad_ref, i, n_row_tiles, jb,
               *, n_qk_blocks, n_q_blocks, halo):
    tr, tc = cur_ref.shape
    keep_prev = jnp.where(i == 0, 0.0, 1.0)
    keep_next = jnp.where(i == n_row_tiles - 1, 0.0, 1.0)
    pad_ref[0:halo, :] = prev_ref[...].astype(F32) * keep_prev
    pad_ref[halo:halo + tr, :] = cur_ref[...].astype(F32)
    pad_ref[halo + tr:, :] = next_ref[...].astype(F32) * keep_next
    side = (CONV_WIDTH - 1) // 2
    padded = pad_ref[...]
    rows = tr + 2 * halo
    acc = jnp.zeros((tr, tc), F32)
    for tap in range(CONV_WIDTH):
        back = side - tap
        shifted = padded if back == 0 else pltpu.roll(padded, back % rows, 0)
        acc = acc + shifted[halo:halo + tr] * w_ref[tap:tap + 1, :]
    y = acc * jax.nn.sigmoid(acc)
    is_qk = jb < n_qk_blocks
    q_scale = jnp.where(jb < n_q_blocks, GDN_HEAD_DIM ** -0.5, 1.0)
    for g in range(tc // GDN_HEAD_DIM):
        sl = slice(g * GDN_HEAD_DIM, (g + 1) * GDN_HEAD_DIM)
        t = y[:, sl]
        normed = t * lax.rsqrt(jnp.sum(t * t, axis=-1, keepdims=True) + EPS) * q_scale
        o_ref[:, sl] = jnp.where(is_qk, normed, t)


def _conv_kernel(prev_ref, cur_ref, next_ref, w_ref, o_ref, pad_ref, **conv_kw):
    _conv_tile(prev_ref, cur_ref, next_ref, w_ref, o_ref, pad_ref,
               pl.program_id(0), pl.num_programs(0), pl.program_id(1), **conv_kw)


def _gdn_conv(p, off, width, conv_w):
    s = p.shape[0]
    halo = BF16_SUBLANES
    tr = _pick(s, (1024, 512, 256, 128))
    tc = _pick(width // 3, (512, 256, 128))
    ob = off // tc
    n_halo_blocks = s // halo
    return pl.pallas_call(
        functools.partial(_conv_kernel, n_qk_blocks=2 * (width // 3) // tc, n_q_blocks=(width // 3) // tc,
                          halo=halo),
        grid=(s // tr, width // tc),
        in_specs=[pl.BlockSpec((halo, tc), lambda i, j: (jnp.maximum(i * (tr // halo) - 1, 0), ob + j)),
                  pl.BlockSpec((tr, tc), lambda i, j: (i, ob + j)),
                  pl.BlockSpec((halo, tc), lambda i, j: (jnp.minimum((i + 1) * (tr // halo), n_halo_blocks - 1), ob + j)),
                  pl.BlockSpec((CONV_WIDTH, tc), lambda i, j: (0, j))],
        out_specs=pl.BlockSpec((tr, tc), lambda i, j: (i, j)),
        out_shape=jax.ShapeDtypeStruct((s, width), F32),
        scratch_shapes=[pltpu.VMEM((tr + 2 * halo, tc), F32)],
        compiler_params=_params("parallel", "parallel"),
        name="gdn_conv",
    )(p, p, p, conv_w)


def _unit_triangular_inverses(lms):
    c = lms[0].shape[0]
    i = lax.broadcasted_iota(jnp.int32, (c, c), 0)
    j = lax.broadcasted_iota(jnp.int32, (c, c), 1)
    level = jnp.bitwise_xor(i, j)
    eye = jnp.where(i == j, 1.0, 0.0)
    xs = [eye - jnp.where(level == 1, lm, 0.0) for lm in lms]
    span = 2
    while span < c:
        joins = (level >= span) & (level < 2 * span)
        ys = [_dot(jnp.where(joins, lm, 0.0).astype(BF16), x.astype(BF16)) for lm, x in zip(lms, xs)]
        xs = [x - _dot(x.astype(BF16), y.astype(BF16)) for x, y in zip(xs, ys)]
        span *= 2
    resids = [eye - x - _dot_split(lm, x) for lm, x in zip(lms, xs)]
    return [x + _dot(x.astype(BF16), r.astype(BF16)) for x, r in zip(xs, resids)]


def _gdn_prep_kernel(q_ref, k_ref, v_ref, beta_ref, cum_ref, u_ref, w_ref, attn_ref, qg_ref, kd_ref,
                     *, heads):
    c = q_ref.shape[0]
    dk = GDN_HEAD_DIM
    problems = [(direction, hh) for direction in range(2) for hh in range(heads)]
    masks = [_dir_masks(c, direction) for direction in range(2)]
    ks = [k_ref[:, hh * dk:(hh + 1) * dk] for hh in range(heads)]
    kbs = [k.astype(BF16) for k in ks]
    cum_rows = [cum_ref[direction, hh] for direction, hh in problems]
    cums = [_row_to_col(row) for row in cum_rows]
    betas = [_row_to_col(beta_ref[direction, hh]) for direction, hh in problems]
    decays = []
    for (direction, _), cum, row in zip(problems, cums, cum_rows):
        incl = masks[direction][0]
        decays.append(jnp.where(incl, jnp.exp(jnp.where(incl, cum - row, 0.0)), 0.0))
    k_betas = [ks[hh] * beta for (_, hh), beta in zip(problems, betas)]
    lowers = [_dot_nt(k_beta.astype(BF16), kbs[hh]) * jnp.where(masks[direction][1], decay, 0.0)
              for (direction, hh), k_beta, decay in zip(problems, k_betas, decays)]
    invs = _unit_triangular_inverses(lowers)
    e_cums = [jnp.exp(cum) for cum in cums]
    inv_bs = [inv.astype(BF16) for inv in invs]
    us = [_dot(inv_b, (v_ref[:, hh * dk:(hh + 1) * dk] * beta).astype(BF16))
          for (_, hh), inv_b, beta in zip(problems, inv_bs, betas)]
    ws = [_dot(inv_b, (k_beta * e_cum).astype(BF16)) for inv_b, k_beta, e_cum in zip(inv_bs, k_betas, e_cums)]
    for n, (direction, hh) in enumerate(problems):
        sl = slice(hh * dk, (hh + 1) * dk)
        q = q_ref[:, sl]
        u_ref[direction, :, sl] = us[n]
        w_ref[direction, :, sl] = ws[n].astype(w_ref.dtype)
        attn_ref[direction, :, sl] = (_dot_nt(q.astype(BF16), kbs[hh]) * decays[n]).astype(attn_ref.dtype)
        qg_ref[direction, :, sl] = (q * e_cums[n]).astype(qg_ref.dtype)
        last = _lane_pick(cum_rows[n], c - 1 if direction == 0 else 0)
        kd_ref[direction, :, sl] = (ks[hh] * jnp.exp(last - cums[n])).astype(kd_ref.dtype)


def _gdn_scan_kernel(uf_ref, wf_ref, af_ref, qgf_ref, kdf_ref, cumf_ref,
                     ub_ref, wb_ref, ab_ref, qgb_ref, kdb_ref, cumb_ref, of_ref, ob_ref, state_ref,
                     *, heads):
    @pl.when(pl.program_id(1) == 0)
    def _():
        state_ref[...] = jnp.zeros_like(state_ref)

    c = uf_ref.shape[0]
    dk = GDN_HEAD_DIM
    refs = ((uf_ref, wf_ref, af_ref, qgf_ref, kdf_ref, cumf_ref, of_ref),
            (ub_ref, wb_ref, ab_ref, qgb_ref, kdb_ref, cumb_ref, ob_ref))
    chains = [(direction, hh, slice(hh * dk, (hh + 1) * dk)) for direction in range(2) for hh in range(heads)]
    states = [state_ref[direction, hh] for direction, hh, _ in chains]
    sbs = [state.astype(BF16) for state in states]
    v_news = [(refs[direction][0][:, sl] - _dot(refs[direction][1][:, sl], sb)).astype(BF16)
              for (direction, _, sl), sb in zip(chains, sbs)]
    for (direction, hh, sl), state, sb, v_new in zip(chains, states, sbs, v_news):
        _, _, attn_ref, qg_ref, kd_ref, cum_ref, o_ref = refs[direction]
        o_ref[:, sl] = _dot(attn_ref[:, sl], v_new) + _dot(qg_ref[:, sl], sb)
        last = _lane_pick(cum_ref[hh], c - 1 if direction == 0 else 0)
        state_ref[direction, hh] = state * jnp.exp(last) + _dot_tn(kd_ref[:, sl], v_new)


def _gdn(qkv, gates3, n_heads, beta_row0, cum_row0):
    s = qkv.shape[0]
    dk = GDN_HEAD_DIM
    nc = s // CHUNK
    width = n_heads * dk
    heads = _pick(n_heads, (8, 4, 2, 1))
    wb = heads * dk
    gate4 = gates3.reshape(GATE_ROWS // n_heads, n_heads // heads, heads, 1, s)

    def gate_rows(r0):
        return pl.BlockSpec((2, None, heads, 1, CHUNK), lambda h, c: (r0 // (2 * n_heads), h, 0, 0, c))

    cols = [pl.BlockSpec((CHUNK, wb), lambda h, c, o=o: (c, o * (n_heads // heads) + h)) for o in range(3)]
    out_blk = pl.BlockSpec((2, CHUNK, wb), lambda h, c: (0, c, h))
    u, w, attn, qg, kd = pl.pallas_call(
        functools.partial(_gdn_prep_kernel, heads=heads),
        grid=(n_heads // heads, nc),
        in_specs=cols + [gate_rows(beta_row0), gate_rows(cum_row0)],
        out_specs=[out_blk] * 5,
        out_shape=[jax.ShapeDtypeStruct((2, s, width), dt) for dt in (F32, BF16, BF16, BF16, BF16)],
        compiler_params=_params("parallel", "parallel"),
        name="gdn_prep",
    )(qkv, qkv, qkv, gate4, gate4)


    def specs(direction, chunk):
        seq = pl.BlockSpec((None, CHUNK, wb), lambda h, c: (direction, chunk(c), h))
        cum = pl.BlockSpec((None, None, heads, 1, CHUNK),
                           lambda h, c: (cum_row0 // n_heads + direction, h, 0, 0, chunk(c)))
        return [seq] * 5 + [cum]

    out_f, out_b = _both_directions(lambda chunk: pl.BlockSpec((CHUNK, wb), lambda h, c: (chunk(c), h)), nc)
    ops = (u, w, attn, qg, kd, gate4)
    return pl.pallas_call(
        functools.partial(_gdn_scan_kernel, heads=heads),
        grid=(n_heads // heads, nc),
        in_specs=specs(0, lambda c: c) + specs(1, lambda c: nc - 1 - c),
        out_specs=[out_f, out_b],
        out_shape=[jax.ShapeDtypeStruct((s, width), F32)] * 2,
        scratch_shapes=[pltpu.VMEM((2, heads, dk, dk), F32)],
        compiler_params=_params("parallel", "arbitrary"),
        name="gdn_scan",
    )(*ops, *ops)


def _mlstm_kernel(qf_ref, kf_ref, vf_ref, igf_ref, cumf_ref, qb_ref, kb_ref, vb_ref, igb_ref, cumb_ref,
                  of_ref, ob_ref, mem_ref, nrm_ref, m_ref, *, heads):
    @pl.when(pl.program_id(1) == 0)
    def _():
        mem_ref[...] = jnp.zeros_like(mem_ref)
        nrm_ref[...] = jnp.zeros_like(nrm_ref)
        m_ref[...] = jnp.zeros_like(m_ref)

    c = qf_ref.shape[0]
    dk = MLSTM_HEAD_DIM
    refs = ((qf_ref, kf_ref, vf_ref, igf_ref, cumf_ref, of_ref),
            (qb_ref, kb_ref, vb_ref, igb_ref, cumb_ref, ob_ref))
    chain_ids = [(direction, hh) for direction in range(2) for hh in range(heads)]

    def col(hh):
        return slice(hh * dk, (hh + 1) * dk)

    prepared = []
    for direction, hh in chain_ids:
        q_ref, k_ref, _, ig_ref, cum_ref, _ = refs[direction]
        incl, _ = _dir_masks(c, direction)
        ig_row = ig_ref[hh]
        cum_row = cum_ref[hh]
        ig = _row_to_col(ig_row)
        cum = _row_to_col(cum_row)
        log_d = jnp.where(incl, cum - cum_row + ig_row, -jnp.inf)
        d_max = jnp.max(log_d, axis=1, keepdims=True)
        qk = _dot_nt(q_ref[:, col(hh)].astype(BF16), (k_ref[:, col(hh)] * (dk ** -0.5)).astype(BF16))
        b_last = _lane_pick(cum_row, c - 1 if direction == 0 else 0)
        log_w = b_last - cum + ig
        w_max = jnp.max(log_w, axis=0, keepdims=True)
        prepared.append((cum, log_d, d_max, qk, b_last, log_w, w_max))
    m_prevs = [m_ref[direction, hh, 0:1, 0:1] for direction, hh in chain_ids]
    for (direction, hh), (cum, log_d, d_max, qk, *_), m_prev in zip(chain_ids, prepared, m_prevs):
        q_ref, _, v_ref, _, _, o_ref = refs[direction]
        q = q_ref[:, col(hh)]
        m_inter = cum + m_prev
        m_row = jnp.maximum(m_inter, d_max)
        s_mat = qk * jnp.exp(log_d - m_row)
        inter_scale = jnp.exp(m_inter - m_row)
        num = (_dot(s_mat.astype(BF16), v_ref[:, col(hh)].astype(BF16))
               + inter_scale * _dot(q.astype(BF16), mem_ref[direction, hh].astype(BF16)))
        den = (jnp.sum(s_mat, axis=1, keepdims=True)
               + inter_scale * jnp.sum(q * nrm_ref[direction, hh], axis=1, keepdims=True))
        o_ref[:, col(hh)] = num * (1.0 / jnp.maximum(jnp.abs(den), jnp.exp(-m_row)))
    for (direction, hh), (_, _, _, _, b_last, log_w, w_max), m_prev in zip(chain_ids, prepared, m_prevs):
        _, k_ref, v_ref, _, _, _ = refs[direction]
        m_new = jnp.maximum(b_last + m_prev, w_max)
        kw = k_ref[:, col(hh)] * ((dk ** -0.5) * jnp.exp(log_w - m_new))
        carry_scale = jnp.exp(b_last + m_prev - m_new)
        mem_ref[direction, hh] = (carry_scale * mem_ref[direction, hh]
                                  + _dot_tn(kw.astype(BF16), v_ref[:, col(hh)].astype(BF16)))
        nrm_ref[direction, hh] = carry_scale * nrm_ref[direction, hh] + jnp.sum(kw, axis=0, keepdims=True)
        m_ref[direction, hh] = jnp.broadcast_to(m_new, m_ref.shape[2:])


def _mlstm(p, offs, gates3, n_heads, ig_row0, cum_row0):
    s = p.shape[0]
    dk = MLSTM_HEAD_DIM
    nc = s // CHUNK
    heads = _pick(n_heads, (4, 2, 1))
    wb = heads * dk
    qo, ko, vo = (o // wb for o in offs)
    gate4 = gates3.reshape(GATE_ROWS // n_heads, n_heads // heads, heads, 1, s)

    def specs(direction, chunk):
        col = [pl.BlockSpec((CHUNK, wb), lambda h, c, o=o: (chunk(c), o + h)) for o in (qo, ko, vo)]
        rows = [pl.BlockSpec((None, None, heads, 1, CHUNK),
                             lambda h, c, r0=r0: (r0 // n_heads + direction, h, 0, 0, chunk(c)))
                for r0 in (ig_row0, cum_row0)]
        return col + rows

    out_f, out_b = _both_directions(lambda chunk: pl.BlockSpec((CHUNK, wb), lambda h, c: (chunk(c), h)), nc)
    return pl.pallas_call(
        functools.partial(_mlstm_kernel, heads=heads),
        grid=(n_heads // heads, nc),
        in_specs=specs(0, lambda c: c) + specs(1, lambda c: nc - 1 - c),
        out_specs=[out_f, out_b],
        out_shape=[jax.ShapeDtypeStruct((s, n_heads * dk), F32)] * 2,
        scratch_shapes=[pltpu.VMEM((2, heads, dk, dk), F32), pltpu.VMEM((2, heads, 1, dk), F32),
                        pltpu.VMEM((2, heads, 8, 128), F32)],
        compiler_params=_params("parallel", "arbitrary"),
        name="mlstm_scan",
    )(p, p, p, gate4, gate4, p, p, p, gate4, gate4)


def _finalize_kernel(yf_ref, yb_ref, g_ref, w_ref, o_ref, *, head_dim, subtract_mean, swish_gate):
    y = yf_ref[...] + yb_ref[...]
    g = g_ref[...].astype(F32)
    gate = jax.nn.sigmoid(g)
    if swish_gate:
        gate = g * gate
    w = w_ref[...]
    for hh in range(y.shape[1] // head_dim):
        sl = slice(hh * head_dim, (hh + 1) * head_dim)
        t = y[:, sl]
        if subtract_mean:
            t = t - jnp.mean(t, axis=-1, keepdims=True)
        t = t * lax.rsqrt(jnp.mean(t * t, axis=-1, keepdims=True) + EPS)
        o_ref[:, sl] = (t * w[:, sl] * gate[:, sl]).astype(o_ref.dtype)


def _finalize(y2, p, gate_off, norm_w, head_dim, subtract_mean, swish_gate):
    y_f, y_b = y2
    s, width = y_f.shape
    tm = _pick(s, (512, 256, 128))
    tc = _pick(width, (2048, 1024, 512, 256))
    go = gate_off // tc
    blk = pl.BlockSpec((tm, tc), lambda i, j: (i, j))
    return pl.pallas_call(
        functools.partial(_finalize_kernel, head_dim=head_dim, subtract_mean=subtract_mean,
                          swish_gate=swish_gate),
        grid=(s // tm, width // tc),
        in_specs=[blk, blk, pl.BlockSpec((tm, tc), lambda i, j: (i, go + j)),
                  pl.BlockSpec((1, tc), lambda i, j: (0, j))],
        out_specs=blk,
        out_shape=jax.ShapeDtypeStruct((s, width), BF16),
        compiler_params=_params("parallel", "parallel"),
        name="branch_finalize",
    )(y_f, y_b, p, norm_w.reshape(1, width))


def _merge_kernel(ya_ref, yb_ref, yc_ref, wb_ref, la_ref, lb_ref, lc_ref, o_ref):
    acc = jax.nn.sigmoid(la_ref[...].astype(F32)) * _dot(ya_ref[...], wb_ref[0])
    acc = acc + jax.nn.sigmoid(lb_ref[...].astype(F32)) * _dot(yb_ref[...], wb_ref[1])
    acc = acc + jax.nn.sigmoid(lc_ref[...].astype(F32)) * _dot(yc_ref[...], wb_ref[2])
    o_ref[...] = acc.astype(o_ref.dtype)


def _merge(ys, w_branch, layer, p, logit_off, d_model):
    s, width = ys[0].shape
    tm = _pick(s, (1024, 512, 256, 128))
    tn = _pick(d_model, (256, 128))
    nb = d_model // tn
    lo = logit_off // tn
    y_blk = pl.BlockSpec((tm, width), lambda i, j: (i, 0))

    def logit(r):
        return pl.BlockSpec((tm, tn), lambda i, j: (i, lo + r * nb + j))

    return pl.pallas_call(
        _merge_kernel,
        grid=(s // tm, nb),
        in_specs=[y_blk, y_blk, y_blk,
                  pl.BlockSpec((None, N_BRANCHES, width, tn), lambda i, j: (layer, 0, 0, j)),
                  logit(0), logit(1), logit(2)],
        out_specs=pl.BlockSpec((tm, tn), lambda i, j: (i, j)),
        out_shape=jax.ShapeDtypeStruct((s, d_model), BF16),
        compiler_params=_params("parallel", "parallel"),
        name="merge",
    )(*ys, w_branch, p, p, p)


def kernel(x, positions, norm_mix, w_in, conv_w, gdn_a_log, gdn_dt_bias, mlstm_i_bias, mlstm_f_bias,
           ret_norm, gdn_norm, mlstm_norm, w_branch, w_out, norm_ffn, w_ffn_in, w_ffn_out, norm_final):
    batch, s, d = x.shape
    assert batch == 1 and s % CHUNK == 0
    depth = w_in.shape[0]
    mix = d // 2
    ret_heads = mix // RET_HEAD_DIM
    gdn_heads = mix // GDN_HEAD_DIM
    mlstm_heads = mix // MLSTM_HEAD_DIM
    d_ff = w_ffn_out.shape[1]
    assert 4 * gdn_heads + 4 * mlstm_heads <= GATE_ROWS and GATE_ROWS % (2 * gdn_heads) == 0

    sizes = (mix, mix, mix, mix, 3 * mix, mix, 2 * gdn_heads, 2 * gdn_heads,
             mix, mix, mix, mix, 2 * mlstm_heads, 2 * mlstm_heads, N_BRANCHES * d)
    starts = np.concatenate([[0], np.cumsum(sizes)]).tolist()
    n_lead = starts[6]
    small = (6, 7, 12, 13)
    n_small = sum(sizes[idx] for idx in small)

    x2 = x.reshape(s, d)
    tables = _rope_table(positions.reshape(s), RET_HEAD_DIM // 2)
    w_in_t = jnp.swapaxes(w_in, 1, 2)
    w_branch_bf = w_branch.astype(BF16)
    w_ffn_out_bf = w_ffn_out.astype(BF16)

    for l in range(depth):
        w_small_t = jnp.concatenate([w_in_t[l, starts[idx]:starts[idx + 1], :] for idx in small], axis=0)
        w_small_t = jnp.pad(w_small_t, ((0, GATE_ROWS - n_small), (0, 0)))
        zeros = jnp.zeros((2 * gdn_heads,), F32)
        gate_bias = jnp.concatenate([zeros, gdn_dt_bias[l].reshape(-1), mlstm_i_bias[l].reshape(-1),
                                     mlstm_f_bias[l].reshape(-1)])
        gate_scale = jnp.concatenate([zeros + 1.0, -jnp.exp(gdn_a_log[l].reshape(-1)),
                                      jnp.ones((4 * mlstm_heads,), F32)])
        gate_par = jnp.pad(jnp.stack([gate_bias, gate_scale], axis=1), ((0, GATE_ROWS - n_small), (0, 0)))

        h = _rmsnorm(x2, norm_mix[l], BF16)
        p = _matmul_wt(h, w_in_t, l, ((0, n_lead), (starts[8], 4 * mix), (starts[14], N_BRANCHES * d)),
                       "in_proj")
        off_mlstm, off_logits = n_lead, n_lead + 4 * mix
        gates3 = _gates(h, w_small_t, gate_par, gdn_heads, mlstm_heads).reshape(GATE_ROWS, 1, s)

        y_ret = _retention(p, (starts[0], starts[1], starts[2]), tables, ret_heads)
        y_ret = _finalize(y_ret, p, starts[3], ret_norm[l], RET_HEAD_DIM, True, True)

        qkv = _gdn_conv(p, starts[4], 3 * mix, conv_w[l])
        y_gdn = _gdn(qkv, gates3, gdn_heads, 0, 2 * gdn_heads)
        y_gdn = _finalize(y_gdn, p, starts[5], gdn_norm[l], GDN_HEAD_DIM, False, True)

        y_ml = _mlstm(p, (off_mlstm, off_mlstm + mix, off_mlstm + 2 * mix), gates3, mlstm_heads,
                      4 * gdn_heads, 4 * gdn_heads + 2 * mlstm_heads)
        y_ml = _finalize(y_ml, p, off_mlstm + 3 * mix, mlstm_norm[l], MLSTM_HEAD_DIM, False, False)

        merged = _merge((y_ret, y_gdn, y_ml), w_branch_bf, l, p, off_logits, d)
        x2 = _matmul_f32w_residual(merged, w_out, l, x2, "out_proj")

        h = _rmsnorm(x2, norm_ffn[l], BF16)
        act = _ffn_in(h, w_ffn_in, l, d_ff)
        x2 = _matmul_residual(act, w_ffn_out_bf, l, x2, "ffn_out")

    return _rmsnorm(x2, norm_final, x.dtype).reshape(batch, s, d)
```
